```python
import jax, jax.numpy as jnp
from jax import lax
import numpy as np

D_MODEL = 1024
BATCH = 16
SEQ = 4096
DEPTH = 2
DEC_BATCH = 8
DEC_SEQ = 4096
PAST_LEN = 128

A_HEADS = 8
A_HEAD_DIM = 64
A_WIDTH = A_HEADS * A_HEAD_DIM
DECAY_LORA = 64
ICLR_LORA = 64
GATE_LORA = 160
GN_EPS = 64e-5
B_HEADS = 8
Q_RANK = 256
KV_RANK = 128
QK_NOPE = 64
QK_ROPE = 32
V_DIM = 64
B_WIDTH = B_HEADS * V_DIM
ROPE_THETA = 10000.0
Q_BLOCK = 128
D_FF = ((8 * D_MODEL // 3 + 255) // 256) * 256
RMS_EPS = 1e-6

SHIFT_WIDTH = 3 * A_WIDTH + 2 * DECAY_LORA + 2 * ICLR_LORA + GATE_LORA
IN_SPLITS = (2 * D_MODEL, SHIFT_WIDTH, Q_RANK, KV_RANK, QK_ROPE)
N_IN = sum(IN_SPLITS)
A_SPLITS = (A_WIDTH, A_WIDTH, A_WIDTH, DECAY_LORA, DECAY_LORA, ICLR_LORA, ICLR_LORA, GATE_LORA)

kernel_name = "hybrid_rwkv7_mla_gated_encoder"


def _split(t, sizes):
    offs = []
    acc = 0
    for s in sizes[:-1]:
        acc += s
        offs.append(acc)
    return jnp.split(t, offs, axis=-1)


def _rmsnorm(x, g):
    x32 = x.astype(jnp.float32)
    y = x32 * lax.rsqrt(jnp.mean(x32 * x32, axis=-1, keepdims=True) + RMS_EPS)
    return (y * g.astype(jnp.float32)).astype(x.dtype)


def _token_shift(p, mu):
    p_prev = jnp.pad(p[:, :-1], ((0, 0), (1, 0), (0, 0)))
    p_next = jnp.pad(p[:, 1:], ((0, 0), (0, 1), (0, 0)))
    return p + mu[0] * (p_prev - p) + mu[1] * (p_next - p)


def _rope_tables(s):
    pos = jnp.arange(s, dtype=jnp.float32)
    inv_freq = 1.0 / (ROPE_THETA ** (jnp.arange(0, QK_ROPE, 2, dtype=jnp.float32) / QK_ROPE))
    ang = pos[:, None] * inv_freq[None, :]
    ang = jnp.concatenate([ang, ang], axis=-1)
    return jnp.cos(ang), jnp.sin(ang)


def _rope(x, cos, sin):
    x1, x2 = jnp.split(x, 2, axis=-1)
    rot = jnp.concatenate([-x2, x1], axis=-1)
    return x * cos.astype(x.dtype) + rot * sin.astype(x.dtype)


def _rwkv7_scan(r, w, k, v, kk, a, reverse):
    bsz, _, h, n = r.shape

    def step(state, inp):
        r_t, w_t, k_t, v_t, kk_t, a_t = inp
        sa = jnp.einsum('bhvk,bhk->bhv', state, kk_t)
        state = (state * w_t[:, :, None, :]
                 - sa[..., None] * (kk_t * a_t)[:, :, None, :]
                 + v_t[..., None] * k_t[:, :, None, :])
        return state, jnp.einsum('bhvk,bhk->bhv', state, r_t)

    xs = tuple(jnp.swapaxes(t, 0, 1) for t in (r, w, k, v, kk, a))
    s0 = jnp.zeros((bsz, h, n, n), jnp.float32)
    _, o = lax.scan(step, s0, xs, reverse=reverse)
    return jnp.swapaxes(o, 0, 1)


def _rwkv7_branch(p, w2, w0, a2, a0, g2, k_k, k_a, r_k, gn_g, gn_b):
    bsz, s, _ = p.shape
    pr, pk, pv, pwf, pwb, paf, pab, pg = _split(p.astype(jnp.float32), A_SPLITS)

    def heads(t):
        return t.reshape(bsz, s, A_HEADS, A_HEAD_DIM)

    def decay(lora, up, base):
        wl = -jax.nn.softplus(-(base + jnp.tanh(lora) @ up)) - 0.5
        return jnp.exp(-jnp.exp(wl))

    def iclr(lora, up, base):
        return jax.nn.sigmoid(base + lora @ up)

    wf, wb = decay(pwf, w2[0], w0[0]), decay(pwb, w2[1], w0[1])
    af, ab = iclr(paf, a2[0], a0[0]), iclr(pab, a2[1], a0[1])
    g = jax.nn.sigmoid(pg) @ g2
    kk = heads(pk * k_k)
    kk = kk / jnp.maximum(jnp.sqrt(jnp.sum(kk * kk, axis=-1, keepdims=True)), 1e-12)
    kf = pk * (1.0 + (af - 1.0) * k_a)
    kb = pk * (1.0 + (ab - 1.0) * k_a)
    r, v = heads(pr), heads(pv)
    o = (_rwkv7_scan(r, heads(wf), heads(kf), v, kk, heads(af), reverse=False)
         + _rwkv7_scan(r, heads(wb), heads(kb), v, kk, heads(ab), reverse=True))
    mean = jnp.mean(o, axis=-1, keepdims=True)
    var = jnp.mean(jnp.square(o - mean), axis=-1, keepdims=True)
    o = ((o - mean) * lax.rsqrt(var + GN_EPS)).reshape(bsz, s, A_WIDTH) * gn_g + gn_b
    bonus = jnp.sum(r * heads(pk) * r_k, axis=-1, keepdims=True) * v
    return (o + bonus.reshape(bsz, s, A_WIDTH)) * g


def _mla_branch(pq, pkv, pkr, q_norm_g, w_uq, kv_norm_g, w_ukv, cos, sin):
    bsz, s, _ = pq.shape
    q = (_rmsnorm(pq, q_norm_g) @ w_uq).reshape(bsz, s, B_HEADS, QK_NOPE + QK_ROPE)
    q_nope = q[..., :QK_NOPE]
    q_rope = _rope(q[..., QK_NOPE:], cos[:, None, :], sin[:, None, :])
    kv = (_rmsnorm(pkv, kv_norm_g) @ w_ukv).reshape(bsz, s, B_HEADS, QK_NOPE + V_DIM)
    k_nope, v = kv[..., :QK_NOPE], kv[..., QK_NOPE:]
    k_rope = _rope(pkr, cos, sin)
    scale = (QK_NOPE + QK_ROPE) ** -0.5
    nb = s // Q_BLOCK

    def blk(qs):
        qn, qr = qs
        sc = (jnp.einsum('bqhd,bkhd->bhqk', qn, k_nope)
              + jnp.einsum('bqhd,bkd->bhqk', qr, k_rope))
        prob = jax.nn.softmax(sc.astype(jnp.float32) * scale, axis=-1)
        return jnp.einsum('bhqk,bkhd->bqhd', prob.astype(v.dtype), v)

    def qblocks(t):
        return jnp.moveaxis(t.reshape(bsz, nb, Q_BLOCK, B_HEADS, t.shape[-1]), 1, 0)

    o = lax.map(blk, (qblocks(q_nope), qblocks(q_rope)))
    return jnp.moveaxis(o, 0, 1).reshape(bsz, s, B_WIDTH)


def _trunk(x, norm_mix_g, w_in, shift_mu, decay_w2, decay_w0, iclr_a2, iclr_a0, gate_g2,
           k_k, k_a, r_k, gn_g, gn_b, w_oa, q_norm_g, w_uq, kv_norm_g, w_ukv, w_ob,
           w_out, norm_ffn_g, w_gu, w_down, final_norm_g):
    cos, sin = _rope_tables(x.shape[1])
    for l in range(DEPTH):
        h = _rmsnorm(x, norm_mix_g[l])
        gates, p_a, pq, pkv, pkr = _split(h @ w_in[l], IN_SPLITS)
        p_a = _token_shift(p_a, shift_mu[l])
        y_a = _rwkv7_branch(p_a, decay_w2[l], decay_w0[l], iclr_a2[l], iclr_a0[l], gate_g2[l],
                            k_k[l], k_a[l], r_k[l], gn_g[l], gn_b[l]).astype(x.dtype) @ w_oa[l]
        y_b = _mla_branch(pq, pkv, pkr, q_norm_g[l], w_uq[l], kv_norm_g[l], w_ukv[l], cos, sin) @ w_ob[l]
        g_a, g_b = jnp.split(jax.nn.sigmoid(gates), 2, axis=-1)
        x = x + (g_a * y_a + g_b * y_b) @ w_out[l]
        h = _rmsnorm(x, norm_ffn_g[l])
        gt, up = jnp.split(h @ w_gu[l], 2, axis=-1)
        x = x + (jax.nn.silu(gt) * up) @ w_down[l]
    return _rmsnorm(x, final_norm_g)


def setup_inputs(seed: int = 0) -> dict:
    key = jax.random.key(seed)
    ks = jax.random.split(key, 32)
    f32 = jnp.float32

    def nrm(k, shape, scale):
        return jax.random.normal(k, shape, f32) * scale

    return {
        "x_prompt": nrm(ks[0], (BATCH, SEQ, D_MODEL), 1.0),
        "x_sample": nrm(ks[1], (DEC_BATCH, DEC_SEQ, D_MODEL), 1.0),
        "norm_mix_g": 1.0 + nrm(ks[2], (DEPTH, D_MODEL), 0.02),
        "w_in": nrm(ks[3], (DEPTH, D_MODEL, N_IN), D_MODEL ** -0.5),
        "shift_mu": 0.3 + nrm(ks[4], (DEPTH, 2, SHIFT_WIDTH), 0.1),
        "decay_w2": nrm(ks[5], (DEPTH, 2, DECAY_LORA, A_WIDTH), 0.1 * DECAY_LORA ** -0.5),
        "decay_w0": nrm(ks[6], (DEPTH, 2, A_WIDTH), 1.0),
        "iclr_a2": nrm(ks[7], (DEPTH, 2, ICLR_LORA, A_WIDTH), 0.1 * ICLR_LORA ** -0.5),
        "iclr_a0": nrm(ks[8], (DEPTH, 2, A_WIDTH), 0.5),
        "gate_g2": nrm(ks[9], (DEPTH, GATE_LORA, A_WIDTH), GATE_LORA ** -0.5),
        "k_k": 0.85 + nrm(ks[10], (DEPTH, A_WIDTH), 0.05),
        "k_a": 1.0 + nrm(ks[11], (DEPTH, A_WIDTH), 0.05),
        "r_k": nrm(ks[12], (DEPTH, A_HEADS, A_HEAD_DIM), 0.1),
        "gn_g": 1.0 + nrm(ks[13], (DEPTH, A_WIDTH), 0.02),
        "gn_b": nrm(ks[14], (DEPTH, A_WIDTH), 0.02),
        "w_oa": nrm(ks[15], (DEPTH, A_WIDTH, D_MODEL), A_WIDTH ** -0.5),
        "q_norm_g": 1.0 + nrm(ks[16], (DEPTH, Q_RANK), 0.02),
        "w_uq": nrm(ks[17], (DEPTH, Q_RANK, B_HEADS * (QK_NOPE + QK_ROPE)), Q_RANK ** -0.5),
        "kv_norm_g": 1.0 + nrm(ks[18], (DEPTH, KV_RANK), 0.02),
        "w_ukv": nrm(ks[19], (DEPTH, KV_RANK, B_HEADS * (QK_NOPE + V_DIM)), KV_RANK ** -0.5),
        "w_ob": nrm(ks[20], (DEPTH, B_WIDTH, D_MODEL), B_WIDTH ** -0.5),
        "w_out": nrm(ks[21], (DEPTH, D_MODEL, D_MODEL), D_MODEL ** -0.5),
        "norm_ffn_g": 1.0 + nrm(ks[22], (DEPTH, D_MODEL), 0.02),
        "w_gu": nrm(ks[23], (DEPTH, D_MODEL, 2 * D_FF), D_MODEL ** -0.5),
        "w_down": nrm(ks[24], (DEPTH, D_FF, D_MODEL), D_FF ** -0.5),
        "final_norm_g": 1.0 + nrm(ks[25], (D_MODEL,), 0.02),
    }


def reference(x_prompt, x_sample, norm_mix_g, w_in, shift_mu, decay_w2, decay_w0, iclr_a2,
              iclr_a0, gate_g2, k_k, k_a, r_k, gn_g, gn_b, w_oa, q_norm_g, w_uq, kv_norm_g,
              w_ukv, w_ob, w_out, norm_ffn_g, w_gu, w_down, final_norm_g):
    y_prompt = _trunk(x_prompt, norm_mix_g, w_in, shift_mu, decay_w2, decay_w0, iclr_a2, iclr_a0,
                      gate_g2, k_k, k_a, r_k, gn_g, gn_b, w_oa, q_norm_g, w_uq, kv_norm_g, w_ukv,
                      w_ob, w_out, norm_ffn_g, w_gu, w_down, final_norm_g)
    y_sample = _trunk(x_sample, norm_mix_g, w_in, shift_mu, decay_w2, decay_w0, iclr_a2, iclr_a0,
                      gate_g2, k_k, k_a, r_k, gn_g, gn_b, w_oa, q_norm_g, w_uq, kv_norm_g, w_ukv,
                      w_ob, w_out, norm_ffn_g, w_gu, w_down, final_norm_g)
    return (y_prompt, y_sample)
```

```python
import functools

import jax
import jax.numpy as jnp
import numpy as np
from jax import lax
from jax.experimental import pallas as pl
from jax.experimental.pallas import tpu as pltpu

F32 = jnp.float32
BF16 = jnp.bfloat16

D_MODEL = 1024
DEPTH = 2
A_HEADS = 8
A_HEAD_DIM = 64
A_WIDTH = A_HEADS * A_HEAD_DIM
DECAY_LORA = 64
ICLR_LORA = 64
GATE_LORA = 160
GN_EPS = 64e-5
B_HEADS = 8
Q_RANK = 256
KV_RANK = 128
QK_NOPE = 64
QK_ROPE = 32
V_DIM = 64
B_WIDTH = B_HEADS * V_DIM
ROPE_THETA = 10000.0
D_FF = 2816
RMS_EPS = 1e-6
SHIFT_WIDTH = 3 * A_WIDTH + 2 * DECAY_LORA + 2 * ICLR_LORA + GATE_LORA

LANES = 128
HEADS_PER_GROUP = LANES // A_HEAD_DIM
CHUNK = 64
SUPER = 256
HALO = 16
GATES_W = 2 * D_MODEL
MLA_W = 640
RKV_W = 3 * A_WIDTH
LORA_W = 512
PA_W = RKV_W + LORA_W
VMEM_LIMIT = 56 * 1024 * 1024


def _dot(a, b):
    return jnp.dot(a, b, preferred_element_type=F32)


def _dot_nt(a, b):
    return lax.dot_general(a, b, (((1,), (1,)), ((), ())), preferred_element_type=F32)


def _dot_tn(a, b):
    return lax.dot_general(a, b, (((0,), (0,)), ((), ())), preferred_element_type=F32)


def _split_dot(a, b):
    hi = a.astype(BF16)
    lo = (a - hi.astype(F32)).astype(BF16)
    return _dot(hi, b) + _dot(lo, b)


def _softplus(z):
    return jnp.maximum(z, 0.0) + jnp.log1p(jnp.exp(-jnp.abs(z)))


def _params(sem):
    return pltpu.CompilerParams(dimension_semantics=sem, vmem_limit_bytes=VMEM_LIMIT)


def _inproj_kernel(xm_ref, xp_ref, xn_ref, g_ref, wg_ref, wa_ref, mu_ref,
                   gates_ref, mla_ref, rkv_ref, lora_ref, hall_ref, pall_ref, *, tm, seq):
    i = pl.program_id(0)
    g = g_ref[...]

    def norm(x):
        ms = jnp.mean(x * x, axis=-1, keepdims=True)
        return x * lax.rsqrt(ms + RMS_EPS) * g

    t0 = i * tm
    has_prev = ((t0 % seq) != 0).astype(F32)
    has_next = (((t0 + tm) % seq) != 0).astype(F32)
    hm = norm(xm_ref[...]).astype(BF16)
    hall_ref[0:HALO, :] = (norm(xp_ref[...]) * has_prev).astype(BF16)
    hall_ref[HALO:HALO + tm, :] = hm
    hall_ref[HALO + tm:, :] = (norm(xn_ref[...]) * has_next).astype(BF16)

    ga = _dot(hm, wg_ref[...])
    gates_ref[...] = ga[:, :GATES_W].astype(BF16)
    mla_ref[...] = ga[:, GATES_W:].astype(BF16)

    ncol = 512
    for c in range(PA_W // ncol):
        cs = slice(c * ncol, (c + 1) * ncol)
        pall_ref[...] = _dot(hall_ref[...], wa_ref[:, cs])
        mu0 = mu_ref[0:1, cs]
        mu1 = mu_ref[1:2, cs]
        cur = pall_ref[HALO:HALO + tm, :]
        prev = pall_ref[HALO - 1:HALO - 1 + tm, :]
        nxt = pall_ref[HALO + 1:HALO + 1 + tm, :]
        sh = (cur + mu0 * (prev - cur) + mu1 * (nxt - cur)).astype(BF16)
        if c < RKV_W // ncol:
            rkv_ref[:, cs] = sh
        else:
            lora_ref[...] = sh


def _inproj(x, g, wg, wa, mu, *, seq, tm):
    t = x.shape[0]
    nt = t // tm
    hb = tm // HALO
    last = t // HALO - 1
    kern = functools.partial(_inproj_kernel, tm=tm, seq=seq)
    return pl.pallas_call(
        kern,
        grid=(nt,),
        in_specs=[
            pl.BlockSpec((tm, D_MODEL), lambda i: (i, 0)),
            pl.BlockSpec((HALO, D_MODEL), lambda i: (jnp.maximum(i * hb - 1, 0), 0)),
            pl.BlockSpec((HALO, D_MODEL), lambda i: (jnp.minimum((i + 1) * hb, last), 0)),
            pl.BlockSpec((1, D_MODEL), lambda i: (0, 0)),
            pl.BlockSpec((D_MODEL, GATES_W + MLA_W), lambda i: (0, 0)),
            pl.BlockSpec((D_MODEL, PA_W), lambda i: (0, 0)),
            pl.BlockSpec((2, PA_W), lambda i: (0, 0)),
        ],
        out_specs=[
            pl.BlockSpec((tm, GATES_W), lambda i: (i, 0)),
            pl.BlockSpec((tm, MLA_W), lambda i: (i, 0)),
            pl.BlockSpec((tm, RKV_W), lambda i: (i, 0)),
            pl.BlockSpec((tm, LORA_W), lambda i: (i, 0)),
        ],
        out_shape=[
            jax.ShapeDtypeStruct((t, GATES_W), BF16),
            jax.ShapeDtypeStruct((t, MLA_W), BF16),
            jax.ShapeDtypeStruct((t, RKV_W), BF16),
            jax.ShapeDtypeStruct((t, LORA_W), BF16),
        ],
        scratch_shapes=[
            pltpu.VMEM((tm + 2 * HALO, D_MODEL), BF16),
            pltpu.VMEM((tm + 2 * HALO, 512), F32),
        ],
        compiler_params=_params(("parallel",)),
        name="inproj",
    )(x, x, x, g, wg, wa, mu)


def _rwkv_kernel(r_ref, k_ref, v_ref, lo_ref, wup_ref, base_ref, kkw_ref, kaw_ref,
                 msk_ref, bones_ref, o_ref,
                 s_ref, ap_ref, u0_ref, rp_ref, o0_ref, bh_ref, kh_ref, ct_ref):
    d = pl.program_id(0)
    j = pl.program_id(3)
    fwd = d == 0
    n = SUPER

    @pl.when(j == 0)
    def _():
        s_ref[...] = jnp.zeros_like(s_ref)

    m_strict = msk_ref[0]
    m_incl = msk_ref[1]
    m_incl_b = m_incl.astype(BF16)
    m_rest_b = msk_ref[2].astype(BF16)
    bones = bones_ref[...]
    bd = bones.astype(F32)

    r = r_ref[...].astype(F32)
    pk = k_ref[...].astype(F32)
    v_b = v_ref[...]
    lane = lax.broadcasted_iota(jnp.int32, (n, LANES), 1)

    lo = lo_ref[...].astype(F32)
    lo_t = jnp.where(lane < DECAY_LORA, jnp.tanh(lo), lo).astype(BF16)
    up = _dot(lo_t, wup_ref[...]) + base_ref[...]
    wl = -_softplus(-up[:, :LANES]) - 0.5
    lw = -jnp.exp(wl)
    a = jax.nn.sigmoid(up[:, LANES:])

    kkraw = pk * kkw_ref[...]
    ss = _split_dot(kkraw * kkraw, bones)
    kk = kkraw / jnp.maximum(jnp.sqrt(ss), 1e-12)
    kd = pk * (1.0 + (a - 1.0) * kaw_ref[...])
    b = kk * a

    lw_hi = lw.astype(BF16)
    lw_lo = (lw - lw_hi.astype(F32)).astype(BF16)
    lwcat = jnp.concatenate([lw_hi, lw_lo], axis=1)
    ci2 = _dot(m_incl_b, lwcat)
    ci = ci2[:, :LANES] + ci2[:, LANES:]
    cr2 = _dot(m_rest_b, lwcat)
    crest = cr2[:, :LANES] + cr2[:, LANES:]
    ce = ci - lw

    at = -kk * jnp.exp(ce)
    rt = r * jnp.exp(ci)
    e_m = jnp.exp(-ci)
    bt = b * e_m
    kt = kd * e_m
    e_r = jnp.exp(crest)
    bh_ref[...] = (b * e_r).astype(BF16)
    kh_ref[...] = (kd * e_r).astype(BF16)
    ct_ref[...] = ci + crest

    lhs = jnp.concatenate([at, rt], axis=0)
    rhs = jnp.concatenate([bt, kt], axis=0).astype(BF16)
    lane2 = lax.broadcasted_iota(jnp.int32, (2 * n, LANES), 1)

    ap = jnp.zeros((n, LANES), F32)
    u0 = jnp.zeros((n, LANES), F32)
    rp = jnp.zeros((n, LANES), F32)
    o0 = jnp.zeros((n, LANES), F32)
    for h in range(HEADS_PER_GROUP):
        lo_l, hi_l = h * A_HEAD_DIM, (h + 1) * A_HEAD_DIM
        hm2 = (lane2 >= lo_l) & (lane2 < hi_l)
        hm = (lane >= lo_l) & (lane < hi_l)
        g_all = _dot_nt(jnp.where(hm2, lhs, 0.0).astype(BF16), rhs)
        a_ab = g_all[:n, :n] * m_strict
        a_ak = (g_all[:n, n:] * m_strict).astype(BF16)
        a_rb = (g_all[n:, :n] * m_incl).astype(BF16)
        a_rk = (g_all[n:, n:] * m_incl).astype(BF16)
        nm = a_ab
        p = _dot(a_ab.astype(BF16), a_ab.astype(BF16))
        for it in range(4):
            p_b = p.astype(BF16)
            tp = _dot(jnp.concatenate([nm, p], axis=0).astype(BF16), p_b)
            nm = nm + p + tp[:n]
            p = tp[n:]
        nm = nm + p + _dot(nm.astype(BF16), p.astype(BF16))
        nm_b = nm.astype(BF16)

        w = _dot(a_ak, v_b)
        x = jnp.concatenate([at, w], axis=1)
        tx = x + _dot(nm_b, x.astype(BF16))
        rx = _dot(a_rb, tx.astype(BF16))
        o0b = _dot(a_rk, v_b)
        ap = jnp.where(hm, tx[:, :LANES], ap)
        u0 = jnp.where(hm, tx[:, LANES:], u0)
        rp = jnp.where(hm, rt + rx[:, :LANES], rp)
        o0 = jnp.where(hm, rx[:, LANES:] + o0b, o0)

    ap_ref[...] = ap.astype(BF16)
    u0_ref[...] = u0.astype(BF16)
    rp_ref[...] = rp.astype(BF16)
    o0_ref[...] = o0

    s = s_ref[...]
    nchunk = SUPER // CHUNK
    for q in range(nchunk):
        c = jnp.where(fwd, q, nchunk - 1 - q)
        sl = pl.ds(pl.multiple_of(c * CHUNK, CHUNK), CHUNK)
        s_b = s.astype(BF16)
        o_ref[sl, :] = _dot_nt(rp_ref[sl, :], s_b) + o0_ref[sl, :]
        bh_c = bh_ref[sl, :]
        gx = _dot_tn(ap_ref[sl, :], bh_c) * bd
        slx = (_dot_tn(u0_ref[sl, :], bh_c) + _dot_tn(v_ref[sl, :], kh_ref[sl, :])) * bd
        pc = jnp.exp(ct_ref[pl.ds(pl.multiple_of(c * CHUNK, CHUNK), 8), :][0:1, :])
        s = s * pc + _dot(s_b, gx.astype(BF16)) + slx
    s_ref[...] = s


def _rwkv_scan(rkv, lora, wup, base, kkw, kaw, masks, bones, *, nseq, seq):
    t = rkv.shape[0]
    nb = seq // SUPER
    ng = A_WIDTH // LANES

    def tok(d, b, g, j):
        return b * nb + jnp.where(d == 0, j, nb - 1 - j)

    return pl.pallas_call(
        _rwkv_kernel,
        grid=(2, nseq, ng, nb),
        in_specs=[
            pl.BlockSpec((SUPER, LANES), lambda d, b, g, j: (tok(d, b, g, j), g)),
            pl.BlockSpec((SUPER, LANES), lambda d, b, g, j: (tok(d, b, g, j), ng + g)),
            pl.BlockSpec((SUPER, LANES), lambda d, b, g, j: (tok(d, b, g, j), 2 * ng + g)),
            pl.BlockSpec((SUPER, LANES), lambda d, b, g, j: (tok(d, b, g, j), d)),
            pl.BlockSpec((None, None, LANES, 2 * LANES), lambda d, b, g, j: (d, g, 0, 0)),
            pl.BlockSpec((None, None, 1, 2 * LANES), lambda d, b, g, j: (d, g, 0, 0)),
            pl.BlockSpec((1, LANES), lambda d, b, g, j: (0, g)),
            pl.BlockSpec((1, LANES), lambda d, b, g, j: (0, g)),
            pl.BlockSpec((None, 3, SUPER, SUPER), lambda d, b, g, j: (d, 0, 0, 0)),
            pl.BlockSpec((LANES, LANES), lambda d, b, g, j: (0, 0)),
        ],
        out_specs=pl.BlockSpec((None, SUPER, LANES), lambda d, b, g, j: (d, tok(d, b, g, j), g)),
        out_shape=jax.ShapeDtypeStruct((2, t, A_WIDTH), F32),
        scratch_shapes=[
            pltpu.VMEM((LANES, LANES), F32),
            pltpu.VMEM((SUPER, LANES), BF16),
            pltpu.VMEM((SUPER, LANES), BF16),
            pltpu.VMEM((SUPER, LANES), BF16),
            pltpu.VMEM((SUPER, LANES), F32),
            pltpu.VMEM((SUPER, LANES), BF16),
            pltpu.VMEM((SUPER, LANES), BF16),
            pltpu.VMEM((SUPER, LANES), F32),
        ],
        compiler_params=_params(("parallel", "parallel", "parallel", "arbitrary")),
        name="rwkv_scan",
    )(rkv, rkv, rkv, lora, wup, base, kkw, kaw, masks, bones)


def _mla_prep_kernel(mla_ref, qg_ref, kvg_ref, wq1_ref, wq2_ref, wkv_ref, e_ref,
                     cq_ref, sq_ref, ck_ref, sk_ref, q_ref, k_ref, v_ref):
    def norm(x, g):
        ms = jnp.mean(x * x, axis=-1, keepdims=True)
        return x * lax.rsqrt(ms + RMS_EPS) * g

    pq = mla_ref[:, 0:Q_RANK].astype(F32)
    pkv = mla_ref[:, Q_RANK:Q_RANK + KV_RANK].astype(F32)
    pkr = mla_ref[:, Q_RANK + KV_RANK:Q_RANK + KV_RANK + LANES].astype(F32)
    pkrot = mla_ref[:, Q_RANK + KV_RANK + LANES:].astype(F32)

    hq = norm(pq, qg_ref[...]).astype(BF16)
    q1 = _dot(hq, wq1_ref[...])
    q2 = _dot(hq, wq2_ref[...])
    cq = jnp.concatenate([cq_ref[...]] * B_HEADS, axis=1)
    sq = jnp.concatenate([sq_ref[...]] * B_HEADS, axis=1)
    scale = (QK_NOPE + QK_ROPE) ** -0.5
    q_ref[...] = ((q1 * cq + q2 * sq) * scale).astype(BF16)

    hkv = norm(pkv, kvg_ref[...]).astype(BF16)
    kv = _dot(hkv, wkv_ref[...])
    krope = (pkr * ck_ref[...] + pkrot * sk_ref[...]).astype(BF16)
    k_ref[...] = (kv[:, :B_HEADS * LANES] + _dot(krope, e_ref[...])).astype(BF16)
    v_ref[...] = kv[:, B_HEADS * LANES:].astype(BF16)


def _mla_prep(mla, qg, kvg, wq1, wq2, wkv, e, cq, sq, ck, sk, *, seq, tm):
    t = mla.shape[0]
    nt = t // tm
    per = seq // tm
    kw = B_HEADS * LANES
    tab = pl.BlockSpec((tm, LANES), lambda i: (i % per, 0))
    return pl.pallas_call(
        _mla_prep_kernel,
        grid=(nt,),
        in_specs=[
            pl.BlockSpec((tm, MLA_W), lambda i: (i, 0)),
            pl.BlockSpec((1, Q_RANK), lambda i: (0, 0)),
            pl.BlockSpec((1, KV_RANK), lambda i: (0, 0)),
            pl.BlockSpec((Q_RANK, kw), lambda i: (0, 0)),
            pl.BlockSpec((Q_RANK, kw), lambda i: (0, 0)),
            pl.BlockSpec((KV_RANK, kw + B_WIDTH), lambda i: (0, 0)),
            pl.BlockSpec((LANES, kw), lambda i: (0, 0)),
            tab, tab, tab, tab,
        ],
        out_specs=[
            pl.BlockSpec((tm, kw), lambda i: (i, 0)),
            pl.BlockSpec((tm, kw), lambda i: (i, 0)),
            pl.BlockSpec((tm, B_WIDTH), lambda i: (i, 0)),
        ],
        out_shape=[
            jax.ShapeDtypeStruct((t, kw), BF16),
            jax.ShapeDtypeStruct((t, kw), BF16),
            jax.ShapeDtypeStruct((t, B_WIDTH), BF16),
        ],
        compiler_params=_params(("parallel",)),
        name="mla_prep",
    )(mla, qg, kvg, wq1, wq2, wkv, e, cq, sq, ck, sk)


def _attn_kernel(q_ref, k_ref, v_ref, o_ref):
    tq = q_ref.shape[0]
    lane = lax.broadcasted_iota(jnp.int32, (tq, LANES), 1)
    v = v_ref[...]
    out = None
    for h in range(2):
        q = q_ref[:, h * LANES:(h + 1) * LANES]
        k = k_ref[:, h * LANES:(h + 1) * LANES]
        s = _dot_nt(q, k)
        m = jnp.max(s, axis=-1, keepdims=True)
        p = jnp.exp(s - m)
        l = jnp.sum(p, axis=-1, keepdims=True)
        pv = _dot(p.astype(BF16), v) / l
        out = pv if h == 0 else jnp.where(lane < V_DIM, out, pv)
    o_ref[...] = out.astype(BF16)


def _attention(q, k, v, *, nseq, seq, tq):
    t = q.shape[0]
    nq = seq // tq
    npair = B_HEADS // 2
    return pl.pallas_call(
        _attn_kernel,
        grid=(nseq, npair, nq),
        in_specs=[
            pl.BlockSpec((tq, 2 * LANES), lambda b, g, i: (b * nq + i, g)),
            pl.BlockSpec((seq, 2 * LANES), lambda b, g, i: (b, g)),
            pl.BlockSpec((seq, LANES), lambda b, g, i: (b, g)),
        ],
        out_specs=pl.BlockSpec((tq, LANES), lambda b, g, i: (b * nq + i, g)),
        out_shape=jax.ShapeDtypeStruct((t, B_WIDTH), BF16),
        compiler_params=_params(("parallel", "parallel", "arbitrary")),
        name="mla_attention",
    )(q, k, v)


def _mix_kernel(x_ref, of_ref, ob_ref, r_ref, k_ref, v_ref, gl_ref, gates_ref, att_ref,
                g2_ref, woa_ref, wob_ref, wout_ref, rk_ref, gng_ref, gnb_ref,
                bmean_ref, bsum_ref, o_ref):
    o = of_ref[...] + ob_ref[...]
    bmean = bmean_ref[...]
    mean = _split_dot(o, bmean)
    dc = o - mean
    var = _split_dot(dc * dc, bmean)
    on = dc * lax.rsqrt(var + GN_EPS) * gng_ref[...] + gnb_ref[...]
    r = r_ref[...].astype(F32)
    pk = k_ref[...].astype(F32)
    v = v_ref[...].astype(F32)
    bonus = _split_dot(r * pk * rk_ref[...], bsum_ref[...]) * v
    g = _dot(jax.nn.sigmoid(gl_ref[...].astype(F32)).astype(BF16), g2_ref[...])
    ya = _dot(((on + bonus) * g).astype(BF16), woa_ref[...])
    yb = _dot(att_ref[...], wob_ref[...])
    sg = jax.nn.sigmoid(gates_ref[...].astype(F32))
    mix = (sg[:, :D_MODEL] * ya + sg[:, D_MODEL:] * yb).astype(BF16)
    o_ref[...] = x_ref[...] + _dot(mix, wout_ref[...])


def _mix(x, o2, rkv, lora, gates, att, g2, woa, wob, wout, rk, gng, gnb, bmean, bsum, *, tm):
    t = x.shape[0]
    nt = t // tm
    nw = A_WIDTH // LANES // 4

    def const(shape):
        return pl.BlockSpec(shape, lambda i: tuple(0 for _ in shape))

    return pl.pallas_call(
        _mix_kernel,
        grid=(nt,),
        in_specs=[
            pl.BlockSpec((tm, D_MODEL), lambda i: (i, 0)),
            pl.BlockSpec((None, tm, A_WIDTH), lambda i: (0, i, 0)),
            pl.BlockSpec((None, tm, A_WIDTH), lambda i: (1, i, 0)),
            pl.BlockSpec((tm, A_WIDTH), lambda i: (i, 0)),
            pl.BlockSpec((tm, A_WIDTH), lambda i: (i, nw)),
            pl.BlockSpec((tm, A_WIDTH), lambda i: (i, 2 * nw)),
            pl.BlockSpec((tm, 2 * LANES), lambda i: (i, 1)),
            pl.BlockSpec((tm, GATES_W), lambda i: (i, 0)),
            pl.BlockSpec((tm, B_WIDTH), lambda i: (i, 0)),
            const((2 * LANES, A_WIDTH)),
            const((A_WIDTH, D_MODEL)),
            const((B_WIDTH, D_MODEL)),
            const((D_MODEL, D_MODEL)),
            const((1, A_WIDTH)),
            const((1, A_WIDTH)),
            const((1, A_WIDTH)),
            const((A_WIDTH, A_WIDTH)),
            const((A_WIDTH, A_WIDTH)),
        ],
        out_specs=pl.BlockSpec((tm, D_MODEL), lambda i: (i, 0)),
        out_shape=jax.ShapeDtypeStruct((t, D_MODEL), F32),
        compiler_params=_params(("parallel",)),
        name="mix",
    )(x, o2, o2, rkv, rkv, rkv, lora, gates, att, g2, woa, wob, wout, rk, gng, gnb, bmean, bsum)


def _ffn_kernel(x_ref, g_ref, wg_ref, wu_ref, wd_ref, fg_ref, o_ref, *, final, fc):
    x = x_ref[...]
    ms = jnp.mean(x * x, axis=-1, keepdims=True)
    h = (x * lax.rsqrt(ms + RMS_EPS) * g_ref[...]).astype(BF16)
    acc = x
    for c in range(D_FF // fc):
        cs = slice(c * fc, (c + 1) * fc)
        gt = _dot(h, wg_ref[:, cs])
        up = _dot(h, wu_ref[:, cs])
        act = (gt * jax.nn.sigmoid(gt) * up).astype(BF16)
        acc = acc + _dot(act, wd_ref[cs, :])
    if final:
        ms2 = jnp.mean(acc * acc, axis=-1, keepdims=True)
        acc = acc * lax.rsqrt(ms2 + RMS_EPS) * fg_ref[...]
    o_ref[...] = acc


def _ffn(x, g, wg, wu, wd, fg, *, tm, final):
    t = x.shape[0]
    nt = t // tm
    kern = functools.partial(_ffn_kernel, final=final, fc=D_FF // 2)
    return pl.pallas_call(
        kern,
        grid=(nt,),
        in_specs=[
            pl.BlockSpec((tm, D_MODEL), lambda i: (i, 0)),
            pl.BlockSpec((1, D_MODEL), lambda i: (0, 0)),
            pl.BlockSpec((D_MODEL, D_FF), lambda i: (0, 0)),
            pl.BlockSpec((D_MODEL, D_FF), lambda i: (0, 0)),
            pl.BlockSpec((D_FF, D_MODEL), lambda i: (0, 0)),
            pl.BlockSpec((1, D_MODEL), lambda i: (0, 0)),
        ],
        out_specs=pl.BlockSpec((tm, D_MODEL), lambda i: (i, 0)),
        out_shape=jax.ShapeDtypeStruct((t, D_MODEL), F32),
        compiler_params=_params(("parallel",)),
        name="ffn",
    )(x, g, wg, wu, wd, fg)


def _rot_matrix():
    half = QK_ROPE // 2
    m = np.zeros((QK_ROPE, QK_ROPE), np.float32)
    for i in range(half):
        m[i + half, i] = -1.0
        m[i, i + half] = 1.0
    return jnp.asarray(m)


def _scan_masks():
    idx = np.arange(SUPER)
    row, col = idx[:, None], idx[None, :]
    same = (row // CHUNK) == (col // CHUNK)
    out = np.zeros((2, 3, SUPER, SUPER), np.float32)
    for d, before in enumerate((col < row, col > row)):
        out[d, 0] = same & before
        out[d, 1] = same & (before | (row == col))
        out[d, 2] = same & ~(before | (row == col))
    return jnp.asarray(out)


def _block_diag_ones(width, block, value):
    idx = np.arange(width)
    m = ((idx[:, None] // block) == (idx[None, :] // block)).astype(np.float32) * value
    return jnp.asarray(m, dtype=BF16)


def _rope_tables(seq):
    pos = jnp.arange(seq, dtype=F32)
    inv_freq = 1.0 / (ROPE_THETA ** (jnp.arange(0, QK_ROPE, 2, dtype=F32) / QK_ROPE))
    ang = pos[:, None] * inv_freq[None, :]
    ang = jnp.concatenate([ang, ang], axis=-1)
    cos, sin = jnp.cos(ang), jnp.sin(ang)
    z = lambda w: jnp.zeros((seq, w), F32)
    cq = jnp.concatenate([jnp.ones((seq, QK_NOPE), F32), cos, z(LANES - QK_NOPE - QK_ROPE)], axis=1)
    sq = jnp.concatenate([z(QK_NOPE), sin, z(LANES - QK_NOPE - QK_ROPE)], axis=1)
    ck = jnp.concatenate([cos, z(LANES - QK_ROPE)], axis=1)
    sk = jnp.concatenate([sin, z(LANES - QK_ROPE)], axis=1)
    return cq, sq, ck, sk


def _prep_layer(l, w_in, shift_mu, decay_w2, decay_w0, iclr_a2, iclr_a0, gate_g2, w_uq, w_ukv):
    rot = _rot_matrix()
    w = w_in[l]
    off = 2 * D_MODEL
    w_gates = w[:, :off]
    w_pa = w[:, off:off + SHIFT_WIDTH]
    mu = shift_mu[l]
    o2 = off + SHIFT_WIDTH
    w_pq = w[:, o2:o2 + Q_RANK]
    w_pkv = w[:, o2 + Q_RANK:o2 + Q_RANK + KV_RANK]
    w_pkr = w[:, o2 + Q_RANK + KV_RANK:]
    zpad = jnp.zeros((D_MODEL, LANES - QK_ROPE), F32)
    w_g = jnp.concatenate([w_gates, w_pq, w_pkv, w_pkr, zpad, w_pkr @ rot, zpad], axis=1)

    def perm_pa(m):
        cuts = np.cumsum([A_WIDTH, A_WIDTH, A_WIDTH, DECAY_LORA, DECAY_LORA, ICLR_LORA, ICLR_LORA]).tolist()
        r, k, v, wf, wb, af, ab, g = jnp.split(m, cuts, axis=1)
        pad = jnp.zeros((m.shape[0], 2 * LANES - GATE_LORA), m.dtype)
        return jnp.concatenate([r, k, v, wf, af, wb, ab, g, pad], axis=1)

    w_a = perm_pa(w_pa)
    mu_p = perm_pa(mu)

    ng = A_WIDTH // LANES
    zl = jnp.zeros((DECAY_LORA, LANES), F32)
    wup = []
    base = []
    for d in range(2):
        rows_d = []
        base_d = []
        for g in range(ng):
            cs = slice(g * LANES, (g + 1) * LANES)
            top = jnp.concatenate([decay_w2[l, d][:, cs], zl], axis=1)
            bot = jnp.concatenate([zl, iclr_a2[l, d][:, cs]], axis=1)
            rows_d.append(jnp.concatenate([top, bot], axis=0))
            base_d.append(jnp.concatenate([decay_w0[l, d][cs], iclr_a0[l, d][cs]])[None, :])
        wup.append(jnp.stack(rows_d))
        base.append(jnp.stack(base_d))
    wup = jnp.stack(wup).astype(BF16)
    base = jnp.stack(base)

    g2 = jnp.concatenate([gate_g2[l], jnp.zeros((2 * LANES - GATE_LORA, A_WIDTH), F32)], axis=0)

    wq = w_uq[l].reshape(Q_RANK, B_HEADS, QK_NOPE + QK_ROPE)
    wq_n, wq_r = wq[..., :QK_NOPE], wq[..., QK_NOPE:]
    zq = jnp.zeros((Q_RANK, B_HEADS, LANES - QK_NOPE - QK_ROPE), F32)
    wq1 = jnp.concatenate([wq_n, wq_r, zq], axis=-1).reshape(Q_RANK, B_HEADS * LANES)
    wq2 = jnp.concatenate([jnp.zeros_like(wq_n), jnp.einsum('qhr,rs->qhs', wq_r, rot), zq],
                          axis=-1).reshape(Q_RANK, B_HEADS * LANES)
    wkv = w_ukv[l].reshape(KV_RANK, B_HEADS, QK_NOPE + V_DIM)
    wk_n, wv = wkv[..., :QK_NOPE], wkv[..., QK_NOPE:]
    zk = jnp.zeros((KV_RANK, B_HEADS, LANES - QK_NOPE), F32)
    wkv_p = jnp.concatenate([jnp.concatenate([wk_n, zk], axis=-1).reshape(KV_RANK, B_HEADS * LANES),
                             wv.reshape(KV_RANK, B_WIDTH)], axis=1)
    return dict(w_g=w_g.astype(BF16), w_a=w_a.astype(BF16), mu=mu_p, wup=wup, base=base,
                g2=g2.astype(BF16), wq1=wq1.astype(BF16), wq2=wq2.astype(BF16),
                wkv=wkv_p.astype(BF16))


def _rope_place():
    e = np.zeros((LANES, B_HEADS * LANES), np.float32)
    for h in range(B_HEADS):
        for i in range(QK_ROPE):
            e[i, h * LANES + QK_NOPE + i] = 1.0
    return jnp.asarray(e, dtype=BF16)


def _trunk(x, nseq, seq, norm_mix_g, w_in, shift_mu, decay_w2, decay_w0, iclr_a2, iclr_a0, gate_g2,
           k_k, k_a, r_k, gn_g, gn_b, w_oa, q_norm_g, w_uq, kv_norm_g, w_ukv, w_ob,
           w_out, norm_ffn_g, w_gu, w_down, final_norm_g, *, tm, tq):
    masks = _scan_masks()
    bones = _block_diag_ones(LANES, A_HEAD_DIM, 1.0)
    bmean = _block_diag_ones(A_WIDTH, A_HEAD_DIM, 1.0 / A_HEAD_DIM)
    bsum = _block_diag_ones(A_WIDTH, A_HEAD_DIM, 1.0)
    e_place = _rope_place()
    cq, sq, ck, sk = _rope_tables(seq)
    for l in range(DEPTH):
        p = _prep_layer(l, w_in, shift_mu, decay_w2, decay_w0, iclr_a2, iclr_a0, gate_g2, w_uq, w_ukv)
        gates, mla, rkv, lora = _inproj(x, norm_mix_g[l][None, :], p["w_g"], p["w_a"], p["mu"],
                                        seq=seq, tm=tm)
        o2 = _rwkv_scan(rkv, lora, p["wup"], p["base"], k_k[l][None, :], k_a[l][None, :],
                        masks, bones, nseq=nseq, seq=seq)
        q, k, v = _mla_prep(mla, q_norm_g[l][None, :], kv_norm_g[l][None, :], p["wq1"], p["wq2"],
                            p["wkv"], e_place, cq, sq, ck, sk, seq=seq, tm=tm)
        att = _attention(q, k, v, nseq=nseq, seq=seq, tq=tq)
        x = _mix(x, o2, rkv, lora, gates, att, p["g2"], w_oa[l].astype(BF16), w_ob[l].astype(BF16),
                 w_out[l].astype(BF16), r_k[l].reshape(1, A_WIDTH), gn_g[l][None, :], gn_b[l][None, :],
                 bmean, bsum, tm=tm)
        x = _ffn(x, norm_ffn_g[l][None, :], w_gu[l][:, :D_FF].astype(BF16), w_gu[l][:, D_FF:].astype(BF16),
                 w_down[l].astype(BF16), final_norm_g[None, :], tm=tm, final=(l == DEPTH - 1))
    return x


def kernel(x_prompt, x_sample, norm_mix_g, w_in, shift_mu, decay_w2, decay_w0, iclr_a2, iclr_a0, gate_g2, k_k, k_a, r_k, gn_g, gn_b, w_oa, q_norm_g, w_uq, kv_norm_g, w_ukv, w_ob, w_out, norm_ffn_g, w_gu, w_down, final_norm_g):
    b1, seq, _ = x_prompt.shape
    b2 = x_sample.shape[0]
    assert x_sample.shape[1] == seq
    x = jnp.concatenate([x_prompt.reshape(b1 * seq, D_MODEL), x_sample.reshape(b2 * seq, D_MODEL)], axis=0)
    y = _trunk(x, b1 + b2, seq, norm_mix_g, w_in, shift_mu, decay_w2, decay_w0, iclr_a2, iclr_a0,
               gate_g2, k_k, k_a, r_k, gn_g, gn_b, w_oa, q_norm_g, w_uq, kv_norm_g, w_ukv, w_ob,
               w_out, norm_ffn_g, w_gu, w_down, final_norm_g, tm=min(512, seq), tq=min(256, seq))
    return (y[:b1 * seq].reshape(b1, seq, D_MODEL), y[b1 * seq:].reshape(b2, seq, D_MODEL))
```

```python
import functools
import math

import jax
import jax.numpy as jnp
import numpy as np
from jax import lax
from jax.experimental import pallas as pl
from jax.experimental.pallas import tpu as pltpu

F32 = jnp.float32
BF16 = jnp.bfloat16

D_MODEL = 1024
DEPTH = 2
A_HEADS = 8
A_HEAD_DIM = 64
A_WIDTH = A_HEADS * A_HEAD_DIM
DECAY_LORA = 64
ICLR_LORA = 64
GATE_LORA = 160
GN_EPS = 64e-5
B_HEADS = 8
Q_RANK = 256
KV_RANK = 128
QK_NOPE = 64
QK_ROPE = 32
V_DIM = 64
B_WIDTH = B_HEADS * V_DIM
ROPE_THETA = 10000.0
D_FF = 2816
RMS_EPS = 1e-6
SHIFT_WIDTH = 3 * A_WIDTH + 2 * DECAY_LORA + 2 * ICLR_LORA + GATE_LORA

LANES = 128
HEADS_PER_GROUP = LANES // A_HEAD_DIM
CHUNK = 64
UNIT = 2 * CHUNK
SCAN_BT = 256
HALO = 16
GATES_W = 2 * D_MODEL
MLA_W = 640
RKV_W = 3 * A_WIDTH
LORA_W = 512
PA_W = RKV_W + LORA_W
VT_ROWS = V_DIM + 16
ATTN_HEADS = 4
ATTN_KCHUNK = 512
VMEM_LIMIT = 56 * 1024 * 1024


def _dot(a, b):
    return jnp.dot(a, b, preferred_element_type=F32)


def _dot_nt(a, b):
    return lax.dot_general(a, b, (((1,), (1,)), ((), ())), preferred_element_type=F32)


def _dot_tn(a, b):
    return lax.dot_general(a, b, (((0,), (0,)), ((), ())), preferred_element_type=F32)


def _split_dot(a, b):
    hi = a.astype(BF16)
    lo = (a - hi.astype(F32)).astype(BF16)
    return _dot(hi, b) + _dot(lo, b)


def _softplus(z):
    return jnp.maximum(z, 0.0) + jnp.log1p(jnp.exp(-jnp.abs(z)))


def _params(sem):
    return pltpu.CompilerParams(dimension_semantics=sem, vmem_limit_bytes=VMEM_LIMIT)


def _inproj_kernel(xm_ref, xp_ref, xn_ref, g_ref, wg_ref, wa_ref, mu_ref,
                   gates_ref, mla_ref, rkv_ref, lora_ref, hall_ref, pall_ref, *, tm, seq):
    i = pl.program_id(0)
    g = g_ref[...]

    def norm(x):
        ms = jnp.mean(x * x, axis=-1, keepdims=True)
        return x * lax.rsqrt(ms + RMS_EPS) * g

    t0 = i * tm
    has_prev = ((t0 % seq) != 0).astype(F32)
    has_next = (((t0 + tm) % seq) != 0).astype(F32)
    hm = norm(xm_ref[...]).astype(BF16)
    hall_ref[0:HALO, :] = (norm(xp_ref[...]) * has_prev).astype(BF16)
    hall_ref[HALO:HALO + tm, :] = hm
    hall_ref[HALO + tm:, :] = (norm(xn_ref[...]) * has_next).astype(BF16)

    ga = _dot(hm, wg_ref[...])
    gates_ref[...] = ga[:, :GATES_W].astype(BF16)
    mla_ref[...] = ga[:, GATES_W:].astype(BF16)

    ncol = 512
    for c in range(PA_W // ncol):
        cs = slice(c * ncol, (c + 1) * ncol)
        pall_ref[...] = _dot(hall_ref[...], wa_ref[:, cs])
        mu0 = mu_ref[0:1, cs]
        mu1 = mu_ref[1:2, cs]
        cur = pall_ref[HALO:HALO + tm, :]
        prev = pall_ref[HALO - 1:HALO - 1 + tm, :]
        nxt = pall_ref[HALO + 1:HALO + 1 + tm, :]
        sh = (cur + mu0 * (prev - cur) + mu1 * (nxt - cur)).astype(BF16)
        if c < RKV_W // ncol:
            rkv_ref[:, cs] = sh
        else:
            lora_ref[...] = sh


def _inproj(x, g, wg, wa, mu, *, seq, tm):
    t = x.shape[0]
    nt = t // tm
    hb = tm // HALO
    last = t // HALO - 1
    kern = functools.partial(_inproj_kernel, tm=tm, seq=seq)
    return pl.pallas_call(
        kern,
        grid=(nt,),
        in_specs=[
            pl.BlockSpec((tm, D_MODEL), lambda i: (i, 0)),
            pl.BlockSpec((HALO, D_MODEL), lambda i: (jnp.maximum(i * hb - 1, 0), 0)),
            pl.BlockSpec((HALO, D_MODEL), lambda i: (jnp.minimum((i + 1) * hb, last), 0)),
            pl.BlockSpec((1, D_MODEL), lambda i: (0, 0)),
            pl.BlockSpec((D_MODEL, GATES_W + MLA_W), lambda i: (0, 0)),
            pl.BlockSpec((D_MODEL, PA_W), lambda i: (0, 0)),
            pl.BlockSpec((2, PA_W), lambda i: (0, 0)),
        ],
        out_specs=[
            pl.BlockSpec((tm, GATES_W), lambda i: (i, 0)),
            pl.BlockSpec((tm, MLA_W), lambda i: (i, 0)),
            pl.BlockSpec((tm, RKV_W), lambda i: (i, 0)),
            pl.BlockSpec((tm, LORA_W), lambda i: (i, 0)),
        ],
        out_shape=[
            jax.ShapeDtypeStruct((t, GATES_W), BF16),
            jax.ShapeDtypeStruct((t, MLA_W), BF16),
            jax.ShapeDtypeStruct((t, RKV_W), BF16),
            jax.ShapeDtypeStruct((t, LORA_W), BF16),
        ],
        scratch_shapes=[
            pltpu.VMEM((tm + 2 * HALO, D_MODEL), BF16),
            pltpu.VMEM((tm + 2 * HALO, 512), F32),
        ],
        compiler_params=_params(("parallel",)),
        name="inproj",
    )(x, x, x, g, wg, wa, mu)


def _rwkv_kernel(rf_ref, kf_ref, vf_ref, lf_ref, rb_ref, kb_ref, vb_ref, lb_ref,
                 wup_ref, base_ref, kkw_ref, kaw_ref, msk_ref, bones_ref,
                 of_ref, ob_ref, s_ref, *, bt):
    j = pl.program_id(2)

    @pl.when(j == 0)
    def _():
        s_ref[...] = jnp.zeros_like(s_ref)

    n, half = UNIT, CHUNK
    nu = bt // UNIT
    bones = bones_ref[...]
    bd = bones.astype(F32)
    kkw = kkw_ref[...]
    kaw = kaw_ref[...]
    lane = lax.broadcasted_iota(jnp.int32, (n, LANES), 1)
    lane2 = lax.broadcasted_iota(jnp.int32, (2 * n, LANES), 1)
    m_strict = [msk_ref[d, 0] for d in range(2)]
    m_incl = [msk_ref[d, 1] for d in range(2)]
    m_incl_b = [m.astype(BF16) for m in m_incl]
    m_rest_b = [msk_ref[d, 2].astype(BF16) for d in range(2)]

    def tile_bd(m_rc):
        return (jnp.concatenate([m_rc, m_rc], axis=0) * bd).astype(BF16)

    chains = []
    for d in range(2):
        refs = ((rf_ref, kf_ref, vf_ref, lf_ref, of_ref) if d == 0
                else (rb_ref, kb_ref, vb_ref, lb_ref, ob_ref))
        for u in (range(nu) if d == 0 else range(nu - 1, -1, -1)):
            chains.append(dict(d=d, rows=slice(u * UNIT, (u + 1) * UNIT), refs=refs))

    for c in chains:
        r_ref, k_ref, v_ref, l_ref, _ = c["refs"]
        rows, d = c["rows"], c["d"]
        lo = l_ref[rows, :].astype(F32)
        lo_t = jnp.where(lane < DECAY_LORA, jnp.tanh(lo), lo).astype(BF16)
        c["up"] = _dot(lo_t, wup_ref[d]) + base_ref[d]
        c["r"] = r_ref[rows, :].astype(F32)
        c["pk"] = k_ref[rows, :].astype(F32)
        c["v"] = v_ref[rows, :]
        c["kkraw"] = c["pk"] * kkw
        c["ss"] = _split_dot(c["kkraw"] * c["kkraw"], bones)

    for c in chains:
        d, up = c["d"], c["up"]
        wl = -_softplus(-up[:, :LANES]) - 0.5
        lw = -jnp.exp(wl)
        c["a"] = jax.nn.sigmoid(up[:, LANES:])
        lw_hi = lw.astype(BF16)
        lw_lo = (lw - lw_hi.astype(F32)).astype(BF16)
        lwcat = jnp.concatenate([lw_hi, lw_lo], axis=1)
        ci2 = _dot(m_incl_b[d], lwcat)
        cr2 = _dot(m_rest_b[d], lwcat)
        c["lw"] = lw
        c["ci"] = ci2[:, :LANES] + ci2[:, LANES:]
        c["crest"] = cr2[:, :LANES] + cr2[:, LANES:]

    for c in chains:
        a, pk, ci, crest = c["a"], c["pk"], c["ci"], c["crest"]
        kk = c["kkraw"] / jnp.maximum(jnp.sqrt(c["ss"]), 1e-12)
        kd = pk * (1.0 + (a - 1.0) * kaw)
        b = kk * a
        at = -kk * jnp.exp(ci - c["lw"])
        rt = c["r"] * jnp.exp(ci)
        e_m = jnp.exp(-ci)
        e_r = jnp.exp(crest)
        c["at"], c["rt"] = at, rt
        c["bh"] = (b * e_r).astype(BF16)
        c["kh"] = (kd * e_r).astype(BF16)
        c["ct"] = ci + crest
        lhs = jnp.concatenate([at, rt], axis=0)
        rhs = jnp.concatenate([b * e_m, kd * e_m], axis=0).astype(BF16)
        c["g"] = [_dot_nt(jnp.where((lane2 < A_HEAD_DIM) == (h == 0), lhs, 0.0).astype(BF16), rhs)
                  for h in range(HEADS_PER_GROUP)]

    heads = []
    for c in chains:
        d = c["d"]
        for h in range(HEADS_PER_GROUP):
            g_all = c["g"][h]
            a_bd = g_all[:n, :n] * m_strict[d]
            a_rc = a_bd[:half] + a_bd[half:]
            heads.append(dict(
                c=c, h=h,
                a_ak=(g_all[:n, n:] * m_strict[d]).astype(BF16),
                a_rb=(g_all[n:, :n] * m_incl[d]).astype(BF16),
                a_rk=(g_all[n:, n:] * m_incl[d]).astype(BF16),
                nm=a_rc,
                p=_dot(a_rc.astype(BF16), a_bd.astype(BF16))))
    for _ in range(4):
        for hd in heads:
            nm, p = hd["nm"], hd["p"]
            tp = _dot(jnp.concatenate([nm, p], axis=0).astype(BF16), tile_bd(p))
            hd["nm"] = nm + p + tp[:half]
            hd["p"] = tp[half:]
    for hd in heads:
        nm, p = hd["nm"], hd["p"]
        hd["nm_bd"] = tile_bd(nm + p + _dot(nm.astype(BF16), tile_bd(p)))

    for hd in heads:
        hd["w"] = _dot(hd["a_ak"], hd["c"]["v"])
    for hd in heads:
        x = jnp.concatenate([hd["c"]["at"], hd["w"]], axis=1)
        hd["tx"] = x + _dot(hd["nm_bd"], x.astype(BF16))
    for hd in heads:
        hd["rx"] = _dot(hd["a_rb"], hd["tx"].astype(BF16))
        hd["o0b"] = _dot(hd["a_rk"], hd["c"]["v"])
    hm = lane < A_HEAD_DIM
    for i, c in enumerate(chains):
        h0, h1 = heads[2 * i], heads[2 * i + 1]
        ap = jnp.where(hm, h0["tx"][:, :LANES], h1["tx"][:, :LANES])
        u0 = jnp.where(hm, h0["tx"][:, LANES:], h1["tx"][:, LANES:])
        rp = c["rt"] + jnp.where(hm, h0["rx"][:, :LANES], h1["rx"][:, :LANES])
        c["o0"] = jnp.where(hm, h0["rx"][:, LANES:] + h0["o0b"], h1["rx"][:, LANES:] + h1["o0b"])
        c["apu"] = jnp.concatenate([ap, u0], axis=1).astype(BF16)
        c["rp"] = rp.astype(BF16)

    for c in chains:
        c["gx"], c["slx"], c["pc"] = [], [], []
        for ck in range(UNIT // CHUNK):
            sl = slice(ck * CHUNK, (ck + 1) * CHUNK)
            gu = _dot_tn(c["apu"][sl], c["bh"][sl])
            vk = _dot_tn(c["v"][sl], c["kh"][sl])
            c["gx"].append((gu[:LANES] * bd).astype(BF16))
            c["slx"].append((gu[LANES:] + vk) * bd)
            c["pc"].append(jnp.exp(c["ct"][ck * CHUNK:ck * CHUNK + 1]))
        c["out"] = [None] * (UNIT // CHUNK)

    s = [s_ref[0], s_ref[1]]
    seqs = [[(c, ck) for c in chains if c["d"] == d for ck in ((0, 1) if d == 0 else (1, 0))]
            for d in range(2)]
    for q in range(len(seqs[0])):
        for d in range(2):
            c, ck = seqs[d][q]
            sl = slice(ck * CHUNK, (ck + 1) * CHUNK)
            s_b = s[d].astype(BF16)
            c["out"][ck] = _dot_nt(c["rp"][sl], s_b) + c["o0"][sl]
            s[d] = s[d] * c["pc"][ck] + _dot(s_b, c["gx"][ck]) + c["slx"][ck]
    for c in chains:
        c["refs"][4][c["rows"], :] = jnp.concatenate(c["out"], axis=0)
    s_ref[0] = s[0]
    s_ref[1] = s[1]


def _rwkv_scan(rkv, lora, wup, base, kkw, kaw, masks, bones, *, nseq, seq, bt):
    t = rkv.shape[0]
    nb = seq // bt
    ng = A_WIDTH // LANES

    def fw(b, g, j):
        return b * nb + j

    def bw(b, g, j):
        return b * nb + nb - 1 - j

    def tok_specs(tokmap, dcol):
        return [
            pl.BlockSpec((bt, LANES), lambda b, g, j: (tokmap(b, g, j), g)),
            pl.BlockSpec((bt, LANES), lambda b, g, j: (tokmap(b, g, j), ng + g)),
            pl.BlockSpec((bt, LANES), lambda b, g, j: (tokmap(b, g, j), 2 * ng + g)),
            pl.BlockSpec((bt, LANES), lambda b, g, j: (tokmap(b, g, j), dcol)),
        ]

    kern = functools.partial(_rwkv_kernel, bt=bt)
    return pl.pallas_call(
        kern,
        grid=(nseq, ng, nb),
        in_specs=tok_specs(fw, 0) + tok_specs(bw, 1) + [
            pl.BlockSpec((2, None, LANES, 2 * LANES), lambda b, g, j: (0, g, 0, 0)),
            pl.BlockSpec((2, None, 1, 2 * LANES), lambda b, g, j: (0, g, 0, 0)),
            pl.BlockSpec((1, LANES), lambda b, g, j: (0, g)),
            pl.BlockSpec((1, LANES), lambda b, g, j: (0, g)),
            pl.BlockSpec((2, 3, UNIT, UNIT), lambda b, g, j: (0, 0, 0, 0)),
            pl.BlockSpec((LANES, LANES), lambda b, g, j: (0, 0)),
        ],
        out_specs=[
            pl.BlockSpec((bt, LANES), lambda b, g, j: (fw(b, g, j), g)),
            pl.BlockSpec((bt, LANES), lambda b, g, j: (bw(b, g, j), g)),
        ],
        out_shape=[jax.ShapeDtypeStruct((t, A_WIDTH), F32), jax.ShapeDtypeStruct((t, A_WIDTH), F32)],
        scratch_shapes=[pltpu.VMEM((2, LANES, LANES), F32)],
        compiler_params=_params(("parallel", "parallel", "arbitrary")),
        name="rwkv_scan",
    )(rkv, rkv, rkv, lora, rkv, rkv, rkv, lora, wup, base, kkw, kaw, masks, bones)


def _mla_prep_kernel(mla_ref, qg_ref, kvg_ref, wq1_ref, wq2_ref, wk_ref, wvt_ref, ones_ref, e_ref,
                     cq_ref, sq_ref, ck_ref, sk_ref, q_ref, k_ref, vt_ref):
    tm = mla_ref.shape[0]

    def norm(x, g):
        ms = jnp.mean(x * x, axis=-1, keepdims=True)
        return x * lax.rsqrt(ms + RMS_EPS) * g

    pq = mla_ref[:, 0:Q_RANK].astype(F32)
    pkv = mla_ref[:, Q_RANK:Q_RANK + KV_RANK].astype(F32)
    pkr = mla_ref[:, Q_RANK + KV_RANK:Q_RANK + KV_RANK + LANES].astype(F32)
    pkrot = mla_ref[:, Q_RANK + KV_RANK + LANES:].astype(F32)

    hq = norm(pq, qg_ref[...]).astype(BF16)
    q1 = _dot(hq, wq1_ref[...])
    q2 = _dot(hq, wq2_ref[...])
    cq = jnp.concatenate([cq_ref[...]] * B_HEADS, axis=1)
    sq = jnp.concatenate([sq_ref[...]] * B_HEADS, axis=1)
    scale = (QK_NOPE + QK_ROPE) ** -0.5 * math.log2(math.e)
    q_ref[...] = ((q1 * cq + q2 * sq) * scale).astype(BF16)

    hkv = norm(pkv, kvg_ref[...]).astype(BF16)
    krope = (pkr * ck_ref[...] + pkrot * sk_ref[...]).astype(BF16)
    k_ref[...] = (_dot(hkv, wk_ref[...]) + _dot(krope, e_ref[...])).astype(BF16)
    ones = jnp.concatenate([ones_ref[...]] * (tm // LANES), axis=1)
    vt_ref[...] = (_dot_nt(wvt_ref[...], hkv) + ones).astype(BF16)


def _mla_prep(mla, qg, kvg, wq1, wq2, wk, wvt, ones, e, cq, sq, ck, sk, *, seq, tm):
    t = mla.shape[0]
    nt = t // tm
    per = seq // tm
    kw = B_HEADS * LANES
    vr = B_HEADS * VT_ROWS
    tab = pl.BlockSpec((tm, LANES), lambda i: (i % per, 0))
    return pl.pallas_call(
        _mla_prep_kernel,
        grid=(nt,),
        in_specs=[
            pl.BlockSpec((tm, MLA_W), lambda i: (i, 0)),
            pl.BlockSpec((1, Q_RANK), lambda i: (0, 0)),
            pl.BlockSpec((1, KV_RANK), lambda i: (0, 0)),
            pl.BlockSpec((Q_RANK, kw), lambda i: (0, 0)),
            pl.BlockSpec((Q_RANK, kw), lambda i: (0, 0)),
            pl.BlockSpec((KV_RANK, kw), lambda i: (0, 0)),
            pl.BlockSpec((vr, KV_RANK), lambda i: (0, 0)),
            pl.BlockSpec((vr, LANES), lambda i: (0, 0)),
            pl.BlockSpec((LANES, kw), lambda i: (0, 0)),
            tab, tab, tab, tab,
        ],
        out_specs=[
            pl.BlockSpec((tm, kw), lambda i: (i, 0)),
            pl.BlockSpec((tm, kw), lambda i: (i, 0)),
            pl.BlockSpec((vr, tm), lambda i: (0, i)),
        ],
        out_shape=[
            jax.ShapeDtypeStruct((t, kw), BF16),
            jax.ShapeDtypeStruct((t, kw), BF16),
            jax.ShapeDtypeStruct((vr, t), BF16),
        ],
        compiler_params=_params(("parallel",)),
        name="mla_prep",
    )(mla, qg, kvg, wq1, wq2, wk, wvt, ones, e, cq, sq, ck, sk)


def _attn_kernel(q_ref, k_ref, vt_ref, o_ref):
    nh = ATTN_HEADS
    tq = q_ref.shape[0]
    kc = min(ATTN_KCHUNK, k_ref.shape[0])
    nc = k_ref.shape[0] // kc

    def scores(h, i):
        s = _dot_nt(k_ref[i * kc:(i + 1) * kc, h * LANES:(h + 1) * LANES],
                    q_ref[:, h * LANES:(h + 1) * LANES])
        return s, jnp.max(s.reshape(kc // 8, 8, tq), axis=0)

    outs = []
    cur = [scores(0, i) for i in range(nc)]
    for h in range(nh):
        m8 = cur[0][1]
        for _, part in cur[1:]:
            m8 = jnp.maximum(m8, part)
        m = jnp.max(m8, axis=0, keepdims=True)
        nxt = []
        ot = None
        for i in range(nc):
            if h + 1 < nh:
                nxt.append(scores(h + 1, i))
            p = jnp.exp2((cur[i][0] - m).astype(BF16))
            part = _dot(vt_ref[h * VT_ROWS:(h + 1) * VT_ROWS, i * kc:(i + 1) * kc], p)
            ot = part if ot is None else ot + part
        outs.append(ot[:V_DIM] / ot[V_DIM:V_DIM + 1])
        cur = nxt
    for g in range(nh // 2):
        o_ref[:, g * LANES:(g + 1) * LANES] = jnp.concatenate(outs[2 * g:2 * g + 2], axis=0).T.astype(BF16)


def _attention(q, k, vt, *, nseq, seq, tq):
    t = q.shape[0]
    nq = seq // tq
    nh = ATTN_HEADS
    ngrp = B_HEADS // nh
    return pl.pallas_call(
        _attn_kernel,
        grid=(nseq, ngrp, nq),
        in_specs=[
            pl.BlockSpec((tq, nh * LANES), lambda b, g, i: (b * nq + i, g)),
            pl.BlockSpec((seq, nh * LANES), lambda b, g, i: (b, g)),
            pl.BlockSpec((nh * VT_ROWS, seq), lambda b, g, i: (g, b)),
        ],
        out_specs=pl.BlockSpec((tq, nh * V_DIM), lambda b, g, i: (b * nq + i, g)),
        out_shape=jax.ShapeDtypeStruct((t, B_WIDTH), BF16),
        compiler_params=_params(("parallel", "parallel", "arbitrary")),
        name="mla_attention",
    )(q, k, vt)


def _mix_kernel(x_ref, of_ref, ob_ref, r_ref, k_ref, v_ref, gl_ref, gates_ref, att_ref,
                g2_ref, woa_ref, wob_ref, wout_ref, rk_ref, gng_ref, gnb_ref,
                bmean_ref, bsum_ref, o_ref):
    o = of_ref[...] + ob_ref[...]
    bmean = bmean_ref[...]
    mean = _split_dot(o, bmean)
    dc = o - mean
    var = _split_dot(dc * dc, bmean)
    on = dc * lax.rsqrt(var + GN_EPS) * gng_ref[...] + gnb_ref[...]
    r = r_ref[...].astype(F32)
    pk = k_ref[...].astype(F32)
    v = v_ref[...].astype(F32)
    bonus = _split_dot(r * pk * rk_ref[...], bsum_ref[...]) * v
    g = _dot(jax.nn.sigmoid(gl_ref[...].astype(F32)).astype(BF16), g2_ref[...])
    ya = _dot(((on + bonus) * g).astype(BF16), woa_ref[...])
    yb = _dot(att_ref[...], wob_ref[...])
    sg = jax.nn.sigmoid(gates_ref[...].astype(F32))
    mix = (sg[:, :D_MODEL] * ya + sg[:, D_MODEL:] * yb).astype(BF16)
    o_ref[...] = x_ref[...] + _dot(mix, wout_ref[...])


def _mix(x, o_f, o_b, rkv, lora, gates, att, g2, woa, wob, wout, rk, gng, gnb, bmean, bsum, *, tm):
    t = x.shape[0]
    nt = t // tm

    def const(shape):
        return pl.BlockSpec(shape, lambda i: tuple(0 for _ in shape))

    return pl.pallas_call(
        _mix_kernel,
        grid=(nt,),
        in_specs=[
            pl.BlockSpec((tm, D_MODEL), lambda i: (i, 0)),
            pl.BlockSpec((tm, A_WIDTH), lambda i: (i, 0)),
            pl.BlockSpec((tm, A_WIDTH), lambda i: (i, 0)),
            pl.BlockSpec((tm, A_WIDTH), lambda i: (i, 0)),
            pl.BlockSpec((tm, A_WIDTH), lambda i: (i, 1)),
            pl.BlockSpec((tm, A_WIDTH), lambda i: (i, 2)),
            pl.BlockSpec((tm, 2 * LANES), lambda i: (i, 1)),
            pl.BlockSpec((tm, GATES_W), lambda i: (i, 0)),
            pl.BlockSpec((tm, B_WIDTH), lambda i: (i, 0)),
            const((2 * LANES, A_WIDTH)),
            const((A_WIDTH, D_MODEL)),
            const((B_WIDTH, D_MODEL)),
            const((D_MODEL, D_MODEL)),
            const((1, A_WIDTH)),
            const((1, A_WIDTH)),
            const((1, A_WIDTH)),
            const((A_WIDTH, A_WIDTH)),
            const((A_WIDTH, A_WIDTH)),
        ],
        out_specs=pl.BlockSpec((tm, D_MODEL), lambda i: (i, 0)),
        out_shape=jax.ShapeDtypeStruct((t, D_MODEL), F32),
        compiler_params=_params(("parallel",)),
        name="mix",
    )(x, o_f, o_b, rkv, rkv, rkv, lora, gates, att, g2, woa, wob, wout, rk, gng, gnb, bmean, bsum)


def _ffn_kernel(x_ref, g_ref, wg_ref, wu_ref, wd_ref, fg_ref, o_ref, *, final, fc):
    x = x_ref[...]
    ms = jnp.mean(x * x, axis=-1, keepdims=True)
    h = (x * lax.rsqrt(ms + RMS_EPS) * g_ref[...]).astype(BF16)
    acc = x
    for c in range(D_FF // fc):
        cs = slice(c * fc, (c + 1) * fc)
        gt = _dot(h, wg_ref[:, cs])
        up = _dot(h, wu_ref[:, cs])
        act = (gt * jax.nn.sigmoid(gt) * up).astype(BF16)
        acc = acc + _dot(act, wd_ref[cs, :])
    if final:
        ms2 = jnp.mean(acc * acc, axis=-1, keepdims=True)
        acc = acc * lax.rsqrt(ms2 + RMS_EPS) * fg_ref[...]
    o_ref[...] = acc


def _ffn(x, g, wg, wu, wd, fg, *, tm, final):
    t = x.shape[0]
    nt = t // tm
    kern = functools.partial(_ffn_kernel, final=final, fc=D_FF // 2)
    return pl.pallas_call(
        kern,
        grid=(nt,),
        in_specs=[
            pl.BlockSpec((tm, D_MODEL), lambda i: (i, 0)),
            pl.BlockSpec((1, D_MODEL), lambda i: (0, 0)),
            pl.BlockSpec((D_MODEL, D_FF), lambda i: (0, 0)),
            pl.BlockSpec((D_MODEL, D_FF), lambda i: (0, 0)),
            pl.BlockSpec((D_FF, D_MODEL), lambda i: (0, 0)),
            pl.BlockSpec((1, D_MODEL), lambda i: (0, 0)),
        ],
        out_specs=pl.BlockSpec((tm, D_MODEL), lambda i: (i, 0)),
        out_shape=jax.ShapeDtypeStruct((t, D_MODEL), F32),
        compiler_params=_params(("parallel",)),
        name="ffn",
    )(x, g, wg, wu, wd, fg)


def _rot_matrix():
    half = QK_ROPE // 2
    m = np.zeros((QK_ROPE, QK_ROPE), np.float32)
    for i in range(half):
        m[i + half, i] = -1.0
        m[i, i + half] = 1.0
    return jnp.asarray(m)


def _scan_masks():
    idx = np.arange(UNIT)
    row, col = idx[:, None], idx[None, :]
    same = (row // CHUNK) == (col // CHUNK)
    out = np.zeros((2, 3, UNIT, UNIT), np.float32)
    for d, before in enumerate((col < row, col > row)):
        out[d, 0] = same & before
        out[d, 1] = same & (before | (row == col))
        out[d, 2] = same & ~(before | (row == col))
    return jnp.asarray(out)


def _block_diag_ones(width, block, value):
    idx = np.arange(width)
    m = ((idx[:, None] // block) == (idx[None, :] // block)).astype(np.float32) * value
    return jnp.asarray(m, dtype=BF16)


def _rope_tables(seq):
    pos = jnp.arange(seq, dtype=F32)
    inv_freq = 1.0 / (ROPE_THETA ** (jnp.arange(0, QK_ROPE, 2, dtype=F32) / QK_ROPE))
    ang = pos[:, None] * inv_freq[None, :]
    ang = jnp.concatenate([ang, ang], axis=-1)
    cos, sin = jnp.cos(ang), jnp.sin(ang)
    z = lambda w: jnp.zeros((seq, w), F32)
    cq = jnp.concatenate([jnp.ones((seq, QK_NOPE), F32), cos, z(LANES - QK_NOPE - QK_ROPE)], axis=1)
    sq = jnp.concatenate([z(QK_NOPE), sin, z(LANES - QK_NOPE - QK_ROPE)], axis=1)
    ck = jnp.concatenate([cos, z(LANES - QK_ROPE)], axis=1)
    sk = jnp.concatenate([sin, z(LANES - QK_ROPE)], axis=1)
    return cq, sq, ck, sk


def _prep_layer(l, w_in, shift_mu, decay_w2, decay_w0, iclr_a2, iclr_a0, gate_g2, w_uq, w_ukv):
    rot = _rot_matrix()
    w = w_in[l]
    off = 2 * D_MODEL
    w_gates = w[:, :off]
    w_pa = w[:, off:off + SHIFT_WIDTH]
    mu = shift_mu[l]
    o2 = off + SHIFT_WIDTH
    w_pq = w[:, o2:o2 + Q_RANK]
    w_pkv = w[:, o2 + Q_RANK:o2 + Q_RANK + KV_RANK]
    w_pkr = w[:, o2 + Q_RANK + KV_RANK:]
    zpad = jnp.zeros((D_MODEL, LANES - QK_ROPE), F32)
    w_g = jnp.concatenate([w_gates, w_pq, w_pkv, w_pkr, zpad, w_pkr @ rot, zpad], axis=1)

    def perm_pa(m):
        cuts = np.cumsum([A_WIDTH, A_WIDTH, A_WIDTH, DECAY_LORA, DECAY_LORA, ICLR_LORA, ICLR_LORA]).tolist()
        r, k, v, wf, wb, af, ab, g = jnp.split(m, cuts, axis=1)
        pad = jnp.zeros((m.shape[0], 2 * LANES - GATE_LORA), m.dtype)
        return jnp.concatenate([r, k, v, wf, af, wb, ab, g, pad], axis=1)

    w_a = perm_pa(w_pa)
    mu_p = perm_pa(mu)

    ng = A_WIDTH // LANES
    zl = jnp.zeros((DECAY_LORA, LANES), F32)
    wup = []
    base = []
    for d in range(2):
        rows_d = []
        base_d = []
        for g in range(ng):
            cs = slice(g * LANES, (g + 1) * LANES)
            top = jnp.concatenate([decay_w2[l, d][:, cs], zl], axis=1)
            bot = jnp.concatenate([zl, iclr_a2[l, d][:, cs]], axis=1)
            rows_d.append(jnp.concatenate([top, bot], axis=0))
            base_d.append(jnp.concatenate([decay_w0[l, d][cs], iclr_a0[l, d][cs]])[None, :])
        wup.append(jnp.stack(rows_d))
        base.append(jnp.stack(base_d))
    wup = jnp.stack(wup).astype(BF16)
    base = jnp.stack(base)

    g2 = jnp.concatenate([gate_g2[l], jnp.zeros((2 * LANES - GATE_LORA, A_WIDTH), F32)], axis=0)

    wq = w_uq[l].reshape(Q_RANK, B_HEADS, QK_NOPE + QK_ROPE)
    wq_n, wq_r = wq[..., :QK_NOPE], wq[..., QK_NOPE:]
    zq = jnp.zeros((Q_RANK, B_HEADS, LANES - QK_NOPE - QK_ROPE), F32)
    wq1 = jnp.concatenate([wq_n, wq_r, zq], axis=-1).reshape(Q_RANK, B_HEADS * LANES)
    wq2 = jnp.concatenate([jnp.zeros_like(wq_n), jnp.einsum('qhr,rs->qhs', wq_r, rot), zq],
                          axis=-1).reshape(Q_RANK, B_HEADS * LANES)
    wkv = w_ukv[l].reshape(KV_RANK, B_HEADS, QK_NOPE + V_DIM)
    wk_n, wv = wkv[..., :QK_NOPE], wkv[..., QK_NOPE:]
    zk = jnp.zeros((KV_RANK, B_HEADS, LANES - QK_NOPE), F32)
    wk_p = jnp.concatenate([wk_n, zk], axis=-1).reshape(KV_RANK, B_HEADS * LANES)
    zv = jnp.zeros((KV_RANK, B_HEADS, VT_ROWS - V_DIM), F32)
    wvt = jnp.concatenate([wv, zv], axis=-1).reshape(KV_RANK, B_HEADS * VT_ROWS).T
    return dict(w_g=w_g.astype(BF16), w_a=w_a.astype(BF16), mu=mu_p, wup=wup, base=base,
                g2=g2.astype(BF16), wq1=wq1.astype(BF16), wq2=wq2.astype(BF16),
                wk=wk_p.astype(BF16), wvt=wvt.astype(BF16))


def _rope_place():
    e = np.zeros((LANES, B_HEADS * LANES), np.float32)
    for h in range(B_HEADS):
        for i in range(QK_ROPE):
            e[i, h * LANES + QK_NOPE + i] = 1.0
    return jnp.asarray(e, dtype=BF16)


def _ones_rows():
    m = np.zeros((B_HEADS, VT_ROWS, LANES), np.float32)
    m[:, V_DIM:, :] = 1.0
    return jnp.asarray(m.reshape(B_HEADS * VT_ROWS, LANES))


def _trunk(x, nseq, seq, norm_mix_g, w_in, shift_mu, decay_w2, decay_w0, iclr_a2, iclr_a0, gate_g2,
           k_k, k_a, r_k, gn_g, gn_b, w_oa, q_norm_g, w_uq, kv_norm_g, w_ukv, w_ob,
           w_out, norm_ffn_g, w_gu, w_down, final_norm_g, *, tm, tq):
    masks = _scan_masks()
    bones = _block_diag_ones(LANES, A_HEAD_DIM, 1.0)
    bmean = _block_diag_ones(A_WIDTH, A_HEAD_DIM, 1.0 / A_HEAD_DIM)
    bsum = _block_diag_ones(A_WIDTH, A_HEAD_DIM, 1.0)
    e_place = _rope_place()
    ones_rows = _ones_rows()
    cq, sq, ck, sk = _rope_tables(seq)
    for l in range(DEPTH):
        p = _prep_layer(l, w_in, shift_mu, decay_w2, decay_w0, iclr_a2, iclr_a0, gate_g2, w_uq, w_ukv)
        gates, mla, rkv, lora = _inproj(x, norm_mix_g[l][None, :], p["w_g"], p["w_a"], p["mu"],
                                        seq=seq, tm=tm)
        o_f, o_b = _rwkv_scan(rkv, lora, p["wup"], p["base"], k_k[l][None, :], k_a[l][None, :],
                              masks, bones, nseq=nseq, seq=seq, bt=min(SCAN_BT, seq))
        q, k, vt = _mla_prep(mla, q_norm_g[l][None, :], kv_norm_g[l][None, :], p["wq1"], p["wq2"],
                             p["wk"], p["wvt"], ones_rows, e_place, cq, sq, ck, sk, seq=seq, tm=tm)
        att = _attention(q, k, vt, nseq=nseq, seq=seq, tq=tq)
        x = _mix(x, o_f, o_b, rkv, lora, gates, att, p["g2"], w_oa[l].astype(BF16), w_ob[l].astype(BF16),
                 w_out[l].astype(BF16), r_k[l].reshape(1, A_WIDTH), gn_g[l][None, :], gn_b[l][None, :],
                 bmean, bsum, tm=tm)
        x = _ffn(x, norm_ffn_g[l][None, :], w_gu[l][:, :D_FF].astype(BF16), w_gu[l][:, D_FF:].astype(BF16),
                 w_down[l].astype(BF16), final_norm_g[None, :], tm=tm, final=(l == DEPTH - 1))
    return x


def kernel(x_prompt, x_sample, norm_mix_g, w_in, shift_mu, decay_w2, decay_w0, iclr_a2, iclr_a0, gate_g2, k_k, k_a, r_k, gn_g, gn_b, w_oa, q_norm_g, w_uq, kv_norm_g, w_ukv, w_ob, w_out, norm_ffn_g, w_gu, w_down, final_norm_g):
    b1, seq, _ = x_prompt.shape
    b2 = x_sample.shape[0]
    assert x_sample.shape[1] == seq
    x = jnp.concatenate([x_prompt.reshape(b1 * seq, D_MODEL), x_sample.reshape(b2 * seq, D_MODEL)], axis=0)
    y = _trunk(x, b1 + b2, seq, norm_mix_g, w_in, shift_mu, decay_w2, decay_w0, iclr_a2, iclr_a0,
               gate_g2, k_k, k_a, r_k, gn_g, gn_b, w_oa, q_norm_g, w_uq, kv_norm_g, w_ukv, w_ob,
               w_out, norm_ffn_g, w_gu, w_down, final_norm_g, tm=min(512, seq), tq=min(256, seq))
    return (y[:b1 * seq].reshape(b1, seq, D_MODEL), y[b1 * seq:].reshape(b2, seq, D_MODEL))
```

```python
import functools
import math

import jax
import jax.numpy as jnp
import numpy as np
from jax import lax
from jax.experimental import pallas as pl
from jax.experimental.pallas import tpu as pltpu

F32 = jnp.float32
BF16 = jnp.bfloat16

D_MODEL = 1024
DEPTH = 2
A_HEADS = 8
A_HEAD_DIM = 64
A_WIDTH = A_HEADS * A_HEAD_DIM
DECAY_LORA = 64
ICLR_LORA = 64
GATE_LORA = 160
GN_EPS = 64e-5
B_HEADS = 8
Q_RANK = 256
KV_RANK = 128
QK_NOPE = 64
QK_ROPE = 32
V_DIM = 64
B_WIDTH = B_HEADS * V_DIM
ROPE_THETA = 10000.0
D_FF = 2816
RMS_EPS = 1e-6
SHIFT_WIDTH = 3 * A_WIDTH + 2 * DECAY_LORA + 2 * ICLR_LORA + GATE_LORA

LANES = 128
HEADS_PER_GROUP = LANES // A_HEAD_DIM
CHUNK = 64
UNIT = 2 * CHUNK
SCAN_BT = 512
HALO = 16
GATES_W = 2 * D_MODEL
MLA_W = 640
RKV_W = 3 * A_WIDTH
LORA_W = 512
PA_W = RKV_W + LORA_W
VT_ROWS = V_DIM + 16
ATTN_HEADS = 4
ATTN_KCHUNK = 512
VMEM_LIMIT = 56 * 1024 * 1024


def _dot(a, b):
    return jnp.dot(a, b, preferred_element_type=F32)


def _dot_nt(a, b):
    return lax.dot_general(a, b, (((1,), (1,)), ((), ())), preferred_element_type=F32)


def _dot_tn(a, b):
    return lax.dot_general(a, b, (((0,), (0,)), ((), ())), preferred_element_type=F32)


def _params(sem):
    return pltpu.CompilerParams(dimension_semantics=sem, vmem_limit_bytes=VMEM_LIMIT)


def _inproj_kernel(xm_ref, xp_ref, xn_ref, g_ref, wg_ref, wa_ref, mu_ref,
                   gates_ref, mla_ref, rkv_ref, lora_ref, hall_ref, pall_ref, *, tm, seq):
    i = pl.program_id(0)
    g = g_ref[...]

    def norm(x):
        ms = jnp.mean(x * x, axis=-1, keepdims=True)
        return x * lax.rsqrt(ms + RMS_EPS) * g

    t0 = i * tm
    has_prev = ((t0 % seq) != 0).astype(F32)
    has_next = (((t0 + tm) % seq) != 0).astype(F32)
    hm = norm(xm_ref[...]).astype(BF16)
    hall_ref[0:HALO, :] = (norm(xp_ref[...]) * has_prev).astype(BF16)
    hall_ref[HALO:HALO + tm, :] = hm
    hall_ref[HALO + tm:, :] = (norm(xn_ref[...]) * has_next).astype(BF16)

    ga = _dot(hm, wg_ref[...])
    gates_ref[...] = ga[:, :GATES_W].astype(BF16)
    mla_ref[...] = ga[:, GATES_W:].astype(BF16)

    ncol = 512
    for c in range(PA_W // ncol):
        cs = slice(c * ncol, (c + 1) * ncol)
        pall_ref[...] = _dot(hall_ref[...], wa_ref[:, cs])
        mu0 = mu_ref[0:1, cs]
        mu1 = mu_ref[1:2, cs]
        cur = pall_ref[HALO:HALO + tm, :]
        prev = pall_ref[HALO - 1:HALO - 1 + tm, :]
        nxt = pall_ref[HALO + 1:HALO + 1 + tm, :]
        sh = (cur + mu0 * (prev - cur) + mu1 * (nxt - cur)).astype(BF16)
        if c < RKV_W // ncol:
            rkv_ref[:, cs] = sh
        else:
            lora_ref[...] = sh


def _inproj(x, g, wg, wa, mu, *, seq, tm):
    t = x.shape[0]
    nt = t // tm
    hb = tm // HALO
    last = t // HALO - 1
    kern = functools.partial(_inproj_kernel, tm=tm, seq=seq)
    return pl.pallas_call(
        kern,
        grid=(nt,),
        in_specs=[
            pl.BlockSpec((tm, D_MODEL), lambda i: (i, 0)),
            pl.BlockSpec((HALO, D_MODEL), lambda i: (jnp.maximum(i * hb - 1, 0), 0)),
            pl.BlockSpec((HALO, D_MODEL), lambda i: (jnp.minimum((i + 1) * hb, last), 0)),
            pl.BlockSpec((1, D_MODEL), lambda i: (0, 0)),
            pl.BlockSpec((D_MODEL, GATES_W + MLA_W), lambda i: (0, 0)),
            pl.BlockSpec((D_MODEL, PA_W), lambda i: (0, 0)),
            pl.BlockSpec((2, PA_W), lambda i: (0, 0)),
        ],
        out_specs=[
            pl.BlockSpec((tm, GATES_W), lambda i: (i, 0)),
            pl.BlockSpec((tm, MLA_W), lambda i: (i, 0)),
            pl.BlockSpec((tm, RKV_W), lambda i: (i, 0)),
            pl.BlockSpec((tm, LORA_W), lambda i: (i, 0)),
        ],
        out_shape=[
            jax.ShapeDtypeStruct((t, GATES_W), BF16),
            jax.ShapeDtypeStruct((t, MLA_W), BF16),
            jax.ShapeDtypeStruct((t, RKV_W), BF16),
            jax.ShapeDtypeStruct((t, LORA_W), BF16),
        ],
        scratch_shapes=[
            pltpu.VMEM((tm + 2 * HALO, D_MODEL), BF16),
            pltpu.VMEM((tm + 2 * HALO, 512), F32),
        ],
        compiler_params=_params(("parallel",)),
        name="inproj",
    )(x, x, x, g, wg, wa, mu)


def _rwkv_kernel(rf_ref, kf_ref, vf_ref, lf_ref, rb_ref, kb_ref, vb_ref, lb_ref,
                 wup_ref, base_ref, kkw_ref, kaw_ref, msk_ref, bones_ref,
                 of_ref, ob_ref, s_ref, rp_sv, o0_sv, gx_sv, slx_sv, pc_sv, *, bt, nb):
    i = pl.program_id(0)

    @pl.when(i == 0)
    def _():
        for ref in (s_ref, rp_sv, o0_sv, gx_sv, slx_sv, pc_sv):
            ref[...] = jnp.zeros_like(ref)

    n, half = UNIT, CHUNK
    nu = bt // UNIT
    bones = bones_ref[...]
    bd = bones.astype(F32)
    kkw = kkw_ref[...]
    kaw = kaw_ref[...]
    lane = lax.broadcasted_iota(jnp.int32, (n, LANES), 1)
    lane2 = lax.broadcasted_iota(jnp.int32, (2 * n, LANES), 1)
    m_strict = [msk_ref[d, 0] for d in range(2)]
    m_incl = [msk_ref[d, 1] for d in range(2)]
    m_incl_b = [m.astype(BF16) for m in m_incl]
    zero_b = jnp.zeros((n, n), BF16)
    zrow = jnp.zeros((CHUNK, LANES), BF16)

    def tile_bd(m_rc):
        return (jnp.concatenate([m_rc, m_rc], axis=0) * bd).astype(BF16)

    def pair_bd(m0, m1):
        return jnp.concatenate([jnp.concatenate([m0, zero_b], axis=1),
                                jnp.concatenate([zero_b, m1], axis=1)], axis=0)

    def tile_bd2(m_rc2):
        return pair_bd(tile_bd(m_rc2[:, :LANES]), tile_bd(m_rc2[:, LANES:]))

    def head_split(x):
        lane_x = lax.broadcasted_iota(jnp.int32, x.shape, 1)
        in_h0 = (lane_x & (LANES - 1)) < A_HEAD_DIM
        return jnp.concatenate([jnp.where(in_h0, x, 0.0), jnp.where(in_h0, 0.0, x)], axis=0).astype(BF16)

    def by_chunk(m):
        return jnp.concatenate([jnp.concatenate([m[:CHUNK], zrow], axis=0),
                                jnp.concatenate([zrow, m[CHUNK:]], axis=0)], axis=1)

    def st_preact(c):
        r_ref, k_ref, v_ref, l_ref, _ = c["refs"]
        rows, d = c["rows"], c["d"]
        lo = l_ref[rows, :].astype(F32)
        lo_t = jnp.where(lane < DECAY_LORA, jnp.tanh(lo), lo).astype(BF16)
        c["up"] = _dot(lo_t, wup_ref[d]) + base_ref[d]
        c["r"] = r_ref[rows, :].astype(F32)
        c["pk"] = k_ref[rows, :].astype(F32)
        c["v"] = v_ref[rows, :]
        c["kkraw"] = c["pk"] * kkw
        c["ss"] = _dot((c["kkraw"] * c["kkraw"]).astype(BF16), bones)

    def st_decay(c):
        d, up = c["d"], c["up"]
        sg = jax.nn.sigmoid(up)
        lw = -math.exp(-0.5) * sg[:, :LANES]
        c["a"] = sg[:, LANES:]
        lw_hi = lw.astype(BF16)
        lw_lo = (lw - lw_hi.astype(F32)).astype(BF16)
        ci2 = _dot(m_incl_b[d], jnp.concatenate([lw_hi, lw_lo], axis=1))
        ci = ci2[:, :LANES] + ci2[:, LANES:]
        ends = [ck * CHUNK + (CHUNK - 1 if d == 0 else 0) for ck in range(UNIT // CHUNK)]
        c["ctot"] = [ci[e:e + 1, :] for e in ends]
        c["lw"], c["ci"] = lw, ci
        c["crest"] = jnp.concatenate([jnp.broadcast_to(t, (CHUNK, LANES)) for t in c["ctot"]], axis=0) - ci

    def st_interact(c):
        a, pk, ci, crest = c["a"], c["pk"], c["ci"], c["crest"]
        kk = c["kkraw"] * lax.rsqrt(jnp.maximum(c["ss"], 1e-24))
        kd = pk * (1.0 + (a - 1.0) * kaw)
        b = kk * a
        at = -kk * jnp.exp(ci - c["lw"])
        rt = c["r"] * jnp.exp(ci)
        e_m = jnp.exp(-ci)
        e_r = jnp.exp(crest)
        c["at"], c["rt"] = at, rt
        c["bh"] = (b * e_r).astype(BF16)
        c["kh"] = (kd * e_r).astype(BF16)
        lhs = jnp.concatenate([at, rt], axis=0)
        rhs = jnp.concatenate([b * e_m, kd * e_m], axis=0).astype(BF16)
        lhs2 = jnp.concatenate([jnp.where(lane2 < A_HEAD_DIM, lhs, 0.0),
                                jnp.where(lane2 < A_HEAD_DIM, 0.0, lhs)], axis=0).astype(BF16)
        c["g"] = _dot_nt(lhs2, rhs)

    def st_masks(c):
        d, g2 = c["d"], c["g"]
        blk = lambda h, i, k: g2[(2 * h + i) * n:(2 * h + i + 1) * n, k * n:(k + 1) * n]
        a_bd = [blk(h, 0, 0) * m_strict[d] for h in range(2)]
        c["a_akrk"] = jnp.concatenate(
            [jnp.concatenate([blk(h, 0, 1) * m_strict[d] for h in range(2)], axis=1),
             jnp.concatenate([blk(h, 1, 1) * m_incl[d] for h in range(2)], axis=1)], axis=0).astype(BF16)
        c["a_rb"] = jnp.concatenate([blk(h, 1, 0) * m_incl[d] for h in range(2)], axis=1).astype(BF16)
        a_rc = jnp.concatenate([a[:half] + a[half:] for a in a_bd], axis=1)
        c["nm"] = a_rc
        c["p"] = _dot(a_rc.astype(BF16), pair_bd(a_bd[0].astype(BF16), a_bd[1].astype(BF16)))

    def st_level(c):
        nm, p = c["nm"], c["p"]
        tp = _dot(jnp.concatenate([nm, p], axis=0).astype(BF16), tile_bd2(p))
        c["nm"] = nm + p + tp[:half]
        c["p"] = tp[half:]

    def st_last_level(c):
        nm, p = c["nm"], c["p"]
        nm = nm + p + _dot(nm.astype(BF16), tile_bd2(p))
        c["nm_bd"] = jnp.concatenate([tile_bd(nm[:, :LANES]), tile_bd(nm[:, LANES:])], axis=1)

    def st_wo(c):
        wo = _dot(c["a_akrk"], head_split(c["v"].astype(F32)))
        c["w"], c["o0b"] = wo[:n], wo[n:]

    def st_tx(c):
        x = jnp.concatenate([c["at"], c["w"]], axis=1)
        c["tx"] = x + _dot(c["nm_bd"], head_split(x))

    def st_rx(c):
        rx = _dot(c["a_rb"], head_split(c["tx"]))
        c["o0"] = rx[:, LANES:] + c["o0b"]
        c["apu"] = c["tx"].astype(BF16)
        c["rp"] = (c["rt"] + rx[:, :LANES]).astype(BF16)

    def st_trans(c):
        gu = _dot_tn(c["apu"], by_chunk(c["bh"]))
        vk = _dot_tn(c["v"], by_chunk(c["kh"]))
        c["gx"] = [(gu[:LANES, ck * LANES:(ck + 1) * LANES] * bd).astype(BF16) for ck in range(2)]
        c["slx"] = [(gu[LANES:, ck * LANES:(ck + 1) * LANES] + vk[:, ck * LANES:(ck + 1) * LANES]) * bd
                    for ck in range(2)]
        c["pc"] = [jnp.exp(t) for t in c["ctot"]]

    nck = UNIT // CHUNK
    groups = []
    for d in range(2):
        refs = ((rf_ref, kf_ref, vf_ref, lf_ref, of_ref) if d == 0
                else (rb_ref, kb_ref, vb_ref, lb_ref, ob_ref))
        groups.append([dict(d=d, u=u, rows=slice(u * UNIT, (u + 1) * UNIT), refs=refs)
                       for u in (range(nu) if d == 0 else range(nu - 1, -1, -1))])

    stage_list = [st_preact, st_decay, st_interact, st_masks] + [st_level] * 4 + [
        st_last_level, st_wo, st_tx, st_rx, st_trans]
    work = [(st, c) for st in stage_list for grp in groups for c in grp]
    first = lax.rem(jnp.maximum(i - 1, 0), nb) == 0
    s = [jnp.where(first, 0.0, s_ref[d]) for d in range(2)]
    seqs = [[(c, ck) for c in groups[d] for ck in (range(nck) if d == 0 else range(nck - 1, -1, -1))]
            for d in range(2)]
    rounds = len(seqs[0])
    per_round = -(-len(work) // rounds)
    outs = {}
    for q in range(rounds):
        for d in range(2):
            c, ck = seqs[d][q]
            u = c["u"]
            sl = slice(u * UNIT + ck * CHUNK, u * UNIT + (ck + 1) * CHUNK)
            s_b = s[d].astype(BF16)
            outs[(d, u, ck)] = _dot_nt(rp_sv[d, sl, :], s_b) + o0_sv[d, sl, :]
            s[d] = s[d] * pc_sv[d, u, ck, 0:1, :] + _dot(s_b, gx_sv[d, u, ck]) + slx_sv[d, u, ck]
        for st, c in work[q * per_round:(q + 1) * per_round]:
            st(c)
    for grp in groups:
        for c in grp:
            c["refs"][4][c["rows"], :] = jnp.concatenate(
                [outs[(c["d"], c["u"], ck)] for ck in range(nck)], axis=0)
    s_ref[0] = s[0]
    s_ref[1] = s[1]
    for grp in groups:
        for c in grp:
            d, u = c["d"], c["u"]
            rp_sv[d, c["rows"], :] = c["rp"]
            o0_sv[d, c["rows"], :] = c["o0"]
            for ck in range(nck):
                gx_sv[d, u, ck] = c["gx"][ck]
                slx_sv[d, u, ck] = c["slx"][ck]
                pc_sv[d, u, ck] = jnp.broadcast_to(c["pc"][ck], (8, LANES))


def _rwkv_scan(rkv, lora, wup, base, kkw, kaw, masks, bones, *, nseq, seq, bt):
    t = rkv.shape[0]
    nb = seq // bt
    ng = A_WIDTH // LANES
    nsteps = nseq * ng * nb
    nu = bt // UNIT
    nck = UNIT // CHUNK

    def decode(i):
        return i // (ng * nb), (i // nb) % ng, i % nb

    def cur(i):
        return decode(jnp.minimum(i, nsteps - 1))

    def prev(i):
        return decode(jnp.maximum(i - 1, 0))

    def fw(bgj):
        b, g, j = bgj
        return b * nb + j

    def bw(bgj):
        b, g, j = bgj
        return b * nb + nb - 1 - j

    def tok_specs(tokmap, dcol):
        return [
            pl.BlockSpec((bt, LANES), lambda i: (tokmap(cur(i)), cur(i)[1])),
            pl.BlockSpec((bt, LANES), lambda i: (tokmap(cur(i)), ng + cur(i)[1])),
            pl.BlockSpec((bt, LANES), lambda i: (tokmap(cur(i)), 2 * ng + cur(i)[1])),
            pl.BlockSpec((bt, LANES), lambda i: (tokmap(cur(i)), dcol)),
        ]

    kern = functools.partial(_rwkv_kernel, bt=bt, nb=nb)
    return pl.pallas_call(
        kern,
        grid=(nsteps + 1,),
        in_specs=tok_specs(fw, 0) + tok_specs(bw, 1) + [
            pl.BlockSpec((2, None, LANES, 2 * LANES), lambda i: (0, cur(i)[1], 0, 0)),
            pl.BlockSpec((2, None, 1, 2 * LANES), lambda i: (0, cur(i)[1], 0, 0)),
            pl.BlockSpec((1, LANES), lambda i: (0, cur(i)[1])),
            pl.BlockSpec((1, LANES), lambda i: (0, cur(i)[1])),
            pl.BlockSpec((2, 2, UNIT, UNIT), lambda i: (0, 0, 0, 0)),
            pl.BlockSpec((LANES, LANES), lambda i: (0, 0)),
        ],
        out_specs=[
            pl.BlockSpec((bt, LANES), lambda i: (fw(prev(i)), prev(i)[1])),
            pl.BlockSpec((bt, LANES), lambda i: (bw(prev(i)), prev(i)[1])),
        ],
        out_shape=[jax.ShapeDtypeStruct((t, A_WIDTH), F32), jax.ShapeDtypeStruct((t, A_WIDTH), F32)],
        scratch_shapes=[
            pltpu.VMEM((2, LANES, LANES), F32),
            pltpu.VMEM((2, bt, LANES), BF16),
            pltpu.VMEM((2, bt, LANES), F32),
            pltpu.VMEM((2, nu, nck, LANES, LANES), BF16),
            pltpu.VMEM((2, nu, nck, LANES, LANES), F32),
            pltpu.VMEM((2, nu, nck, 8, LANES), F32),
        ],
        compiler_params=_params(("arbitrary",)),
        name="rwkv_scan",
    )(rkv, rkv, rkv, lora, rkv, rkv, rkv, lora, wup, base, kkw, kaw, masks, bones)


def _mla_prep_kernel(mla_ref, qg_ref, kvg_ref, wq1_ref, wq2_ref, wk_ref, wvt_ref, ones_ref, e_ref,
                     cq_ref, sq_ref, ck_ref, sk_ref, q_ref, k_ref, vt_ref):
    tm = mla_ref.shape[0]

    def norm(x, g):
        ms = jnp.mean(x * x, axis=-1, keepdims=True)
        return x * lax.rsqrt(ms + RMS_EPS) * g

    pq = mla_ref[:, 0:Q_RANK].astype(F32)
    pkv = mla_ref[:, Q_RANK:Q_RANK + KV_RANK].astype(F32)
    pkr = mla_ref[:, Q_RANK + KV_RANK:Q_RANK + KV_RANK + LANES].astype(F32)
    pkrot = mla_ref[:, Q_RANK + KV_RANK + LANES:].astype(F32)

    hq = norm(pq, qg_ref[...]).astype(BF16)
    q1 = _dot(hq, wq1_ref[...])
    q2 = _dot(hq, wq2_ref[...])
    cq = jnp.concatenate([cq_ref[...]] * B_HEADS, axis=1)
    sq = jnp.concatenate([sq_ref[...]] * B_HEADS, axis=1)
    scale = (QK_NOPE + QK_ROPE) ** -0.5 * math.log2(math.e)
    q_ref[...] = ((q1 * cq + q2 * sq) * scale).astype(BF16)

    hkv = norm(pkv, kvg_ref[...]).astype(BF16)
    krope = (pkr * ck_ref[...] + pkrot * sk_ref[...]).astype(BF16)
    k_ref[...] = (_dot(hkv, wk_ref[...]) + _dot(krope, e_ref[...])).astype(BF16)
    ones = jnp.concatenate([ones_ref[...]] * (tm // LANES), axis=1)
    vt_ref[...] = (_dot_nt(wvt_ref[...], hkv) + ones).astype(BF16)


def _mla_prep(mla, qg, kvg, wq1, wq2, wk, wvt, ones, e, cq, sq, ck, sk, *, seq, tm):
    t = mla.shape[0]
    nt = t // tm
    per = seq // tm
    kw = B_HEADS * LANES
    vr = B_HEADS * VT_ROWS
    tab = pl.BlockSpec((tm, LANES), lambda i: (i % per, 0))
    return pl.pallas_call(
        _mla_prep_kernel,
        grid=(nt,),
        in_specs=[
            pl.BlockSpec((tm, MLA_W), lambda i: (i, 0)),
            pl.BlockSpec((1, Q_RANK), lambda i: (0, 0)),
            pl.BlockSpec((1, KV_RANK), lambda i: (0, 0)),
            pl.BlockSpec((Q_RANK, kw), lambda i: (0, 0)),
            pl.BlockSpec((Q_RANK, kw), lambda i: (0, 0)),
            pl.BlockSpec((KV_RANK, kw), lambda i: (0, 0)),
            pl.BlockSpec((vr, KV_RANK), lambda i: (0, 0)),
            pl.BlockSpec((vr, LANES), lambda i: (0, 0)),
            pl.BlockSpec((LANES, kw), lambda i: (0, 0)),
            tab, tab, tab, tab,
        ],
        out_specs=[
            pl.BlockSpec((tm, kw), lambda i: (i, 0)),
            pl.BlockSpec((tm, kw), lambda i: (i, 0)),
            pl.BlockSpec((vr, tm), lambda i: (0, i)),
        ],
        out_shape=[
            jax.ShapeDtypeStruct((t, kw), BF16),
            jax.ShapeDtypeStruct((t, kw), BF16),
            jax.ShapeDtypeStruct((vr, t), BF16),
        ],
        compiler_params=_params(("parallel",)),
        name="mla_prep",
    )(mla, qg, kvg, wq1, wq2, wk, wvt, ones, e, cq, sq, ck, sk)


def _attn_kernel(q_ref, k_ref, vt_ref, o_ref):
    nh = ATTN_HEADS
    tq = q_ref.shape[0]
    kc = min(ATTN_KCHUNK, k_ref.shape[0])
    nc = k_ref.shape[0] // kc

    def scores(h, i):
        s = _dot_nt(k_ref[i * kc:(i + 1) * kc, h * LANES:(h + 1) * LANES],
                    q_ref[:, h * LANES:(h + 1) * LANES])
        return s, jnp.max(s.reshape(kc // 8, 8, tq), axis=0)

    outs = []
    cur = [scores(0, i) for i in range(nc)]
    for h in range(nh):
        m8 = cur[0][1]
        for _, part in cur[1:]:
            m8 = jnp.maximum(m8, part)
        m = jnp.max(m8, axis=0, keepdims=True)
        nxt = []
        ot = None
        for i in range(nc):
            if h + 1 < nh:
                nxt.append(scores(h + 1, i))
            p = jnp.exp2((cur[i][0] - m).astype(BF16))
            part = _dot(vt_ref[h * VT_ROWS:(h + 1) * VT_ROWS, i * kc:(i + 1) * kc], p)
            ot = part if ot is None else ot + part
        outs.append(ot[:V_DIM] / ot[V_DIM:V_DIM + 1])
        cur = nxt
    for g in range(nh // 2):
        o_ref[:, g * LANES:(g + 1) * LANES] = jnp.concatenate(outs[2 * g:2 * g + 2], axis=0).T.astype(BF16)


def _attention(q, k, vt, *, nseq, seq, tq):
    t = q.shape[0]
    nq = seq // tq
    nh = ATTN_HEADS
    ngrp = B_HEADS // nh
    return pl.pallas_call(
        _attn_kernel,
        grid=(nseq, ngrp, nq),
        in_specs=[
            pl.BlockSpec((tq, nh * LANES), lambda b, g, i: (b * nq + i, g)),
            pl.BlockSpec((seq, nh * LANES), lambda b, g, i: (b, g)),
            pl.BlockSpec((nh * VT_ROWS, seq), lambda b, g, i: (g, b)),
        ],
        out_specs=pl.BlockSpec((tq, nh * V_DIM), lambda b, g, i: (b * nq + i, g)),
        out_shape=jax.ShapeDtypeStruct((t, B_WIDTH), BF16),
        compiler_params=_params(("parallel", "parallel", "arbitrary")),
        name="mla_attention",
    )(q, k, vt)


def _mix_kernel(x_ref, of_ref, ob_ref, r_ref, k_ref, v_ref, gl_ref, gates_ref, att_ref,
                g2_ref, woa_ref, wob_ref, wout_ref, rk_ref, gng_ref, gnb_ref,
                bmean_ref, bsum_ref, o_ref):
    o = of_ref[...] + ob_ref[...]
    bmean = bmean_ref[...]
    mean = _dot(o.astype(BF16), bmean)
    dc = o - mean
    var = _dot((dc * dc).astype(BF16), bmean)
    on = dc * lax.rsqrt(var + GN_EPS) * gng_ref[...] + gnb_ref[...]
    r = r_ref[...].astype(F32)
    pk = k_ref[...].astype(F32)
    v = v_ref[...].astype(F32)
    bonus = _dot((r * pk * rk_ref[...]).astype(BF16), bsum_ref[...]) * v
    g = _dot(jax.nn.sigmoid(gl_ref[...].astype(F32)).astype(BF16), g2_ref[...])
    ya = _dot(((on + bonus) * g).astype(BF16), woa_ref[...])
    yb = _dot(att_ref[...], wob_ref[...])
    sg = jax.nn.sigmoid(gates_ref[...].astype(F32))
    mix = (sg[:, :D_MODEL] * ya + sg[:, D_MODEL:] * yb).astype(BF16)
    o_ref[...] = x_ref[...] + _dot(mix, wout_ref[...])


def _mix(x, o_f, o_b, rkv, lora, gates, att, g2, woa, wob, wout, rk, gng, gnb, bmean, bsum, *, tm):
    t = x.shape[0]
    nt = t // tm

    def const(shape):
        return pl.BlockSpec(shape, lambda i: tuple(0 for _ in shape))

    return pl.pallas_call(
        _mix_kernel,
        grid=(nt,),
        in_specs=[
            pl.BlockSpec((tm, D_MODEL), lambda i: (i, 0)),
            pl.BlockSpec((tm, A_WIDTH), lambda i: (i, 0)),
            pl.BlockSpec((tm, A_WIDTH), lambda i: (i, 0)),
            pl.BlockSpec((tm, A_WIDTH), lambda i: (i, 0)),
            pl.BlockSpec((tm, A_WIDTH), lambda i: (i, 1)),
            pl.BlockSpec((tm, A_WIDTH), lambda i: (i, 2)),
            pl.BlockSpec((tm, 2 * LANES), lambda i: (i, 1)),
            pl.BlockSpec((tm, GATES_W), lambda i: (i, 0)),
            pl.BlockSpec((tm, B_WIDTH), lambda i: (i, 0)),
            const((2 * LANES, A_WIDTH)),
            const((A_WIDTH, D_MODEL)),
            const((B_WIDTH, D_MODEL)),
            const((D_MODEL, D_MODEL)),
            const((1, A_WIDTH)),
            const((1, A_WIDTH)),
            const((1, A_WIDTH)),
            const((A_WIDTH, A_WIDTH)),
            const((A_WIDTH, A_WIDTH)),
        ],
        out_specs=pl.BlockSpec((tm, D_MODEL), lambda i: (i, 0)),
        out_shape=jax.ShapeDtypeStruct((t, D_MODEL), F32),
        compiler_params=_params(("parallel",)),
        name="mix",
    )(x, o_f, o_b, rkv, rkv, rkv, lora, gates, att, g2, woa, wob, wout, rk, gng, gnb, bmean, bsum)


def _ffn_kernel(x_ref, g_ref, wg_ref, wu_ref, wd_ref, fg_ref, o_ref, *, final, fc):
    x = x_ref[...]
    ms = jnp.mean(x * x, axis=-1, keepdims=True)
    h = (x * lax.rsqrt(ms + RMS_EPS) * g_ref[...]).astype(BF16)
    acc = x
    for c in range(D_FF // fc):
        cs = slice(c * fc, (c + 1) * fc)
        gt = _dot(h, wg_ref[:, cs])
        up = _dot(h, wu_ref[:, cs])
        act = (gt * jax.nn.sigmoid(gt) * up).astype(BF16)
        acc = acc + _dot(act, wd_ref[cs, :])
    if final:
        ms2 = jnp.mean(acc * acc, axis=-1, keepdims=True)
        acc = acc * lax.rsqrt(ms2 + RMS_EPS) * fg_ref[...]
    o_ref[...] = acc


def _ffn(x, g, wg, wu, wd, fg, *, tm, final):
    t = x.shape[0]
    nt = t // tm
    kern = functools.partial(_ffn_kernel, final=final, fc=D_FF // 2)
    return pl.pallas_call(
        kern,
        grid=(nt,),
        in_specs=[
            pl.BlockSpec((tm, D_MODEL), lambda i: (i, 0)),
            pl.BlockSpec((1, D_MODEL), lambda i: (0, 0)),
            pl.BlockSpec((D_MODEL, D_FF), lambda i: (0, 0)),
            pl.BlockSpec((D_MODEL, D_FF), lambda i: (0, 0)),
            pl.BlockSpec((D_FF, D_MODEL), lambda i: (0, 0)),
            pl.BlockSpec((1, D_MODEL), lambda i: (0, 0)),
        ],
        out_specs=pl.BlockSpec((tm, D_MODEL), lambda i: (i, 0)),
        out_shape=jax.ShapeDtypeStruct((t, D_MODEL), F32),
        compiler_params=_params(("parallel",)),
        name="ffn",
    )(x, g, wg, wu, wd, fg)


def _rot_matrix():
    half = QK_ROPE // 2
    m = np.zeros((QK_ROPE, QK_ROPE), np.float32)
    for i in range(half):
        m[i + half, i] = -1.0
        m[i, i + half] = 1.0
    return jnp.asarray(m)


def _scan_masks():
    idx = np.arange(UNIT)
    row, col = idx[:, None], idx[None, :]
    same = (row // CHUNK) == (col // CHUNK)
    out = np.zeros((2, 2, UNIT, UNIT), np.float32)
    for d, before in enumerate((col < row, col > row)):
        out[d, 0] = same & before
        out[d, 1] = same & (before | (row == col))
    return jnp.asarray(out)


def _block_diag_ones(width, block, value):
    idx = np.arange(width)
    m = ((idx[:, None] // block) == (idx[None, :] // block)).astype(np.float32) * value
    return jnp.asarray(m, dtype=BF16)


def _rope_tables(seq):
    pos = jnp.arange(seq, dtype=F32)
    inv_freq = 1.0 / (ROPE_THETA ** (jnp.arange(0, QK_ROPE, 2, dtype=F32) / QK_ROPE))
    ang = pos[:, None] * inv_freq[None, :]
    ang = jnp.concatenate([ang, ang], axis=-1)
    cos, sin = jnp.cos(ang), jnp.sin(ang)
    z = lambda w: jnp.zeros((seq, w), F32)
    cq = jnp.concatenate([jnp.ones((seq, QK_NOPE), F32), cos, z(LANES - QK_NOPE - QK_ROPE)], axis=1)
    sq = jnp.concatenate([z(QK_NOPE), sin, z(LANES - QK_NOPE - QK_ROPE)], axis=1)
    ck = jnp.concatenate([cos, z(LANES - QK_ROPE)], axis=1)
    sk = jnp.concatenate([sin, z(LANES - QK_ROPE)], axis=1)
    return cq, sq, ck, sk


def _prep_layer(l, w_in, shift_mu, decay_w2, decay_w0, iclr_a2, iclr_a0, gate_g2, w_uq, w_ukv):
    rot = _rot_matrix()
    w = w_in[l]
    off = 2 * D_MODEL
    w_gates = w[:, :off]
    w_pa = w[:, off:off + SHIFT_WIDTH]
    mu = shift_mu[l]
    o2 = off + SHIFT_WIDTH
    w_pq = w[:, o2:o2 + Q_RANK]
    w_pkv = w[:, o2 + Q_RANK:o2 + Q_RANK + KV_RANK]
    w_pkr = w[:, o2 + Q_RANK + KV_RANK:]
    zpad = jnp.zeros((D_MODEL, LANES - QK_ROPE), F32)
    w_g = jnp.concatenate([w_gates, w_pq, w_pkv, w_pkr, zpad, w_pkr @ rot, zpad], axis=1)

    def perm_pa(m):
        cuts = np.cumsum([A_WIDTH, A_WIDTH, A_WIDTH, DECAY_LORA, DECAY_LORA, ICLR_LORA, ICLR_LORA]).tolist()
        r, k, v, wf, wb, af, ab, g = jnp.split(m, cuts, axis=1)
        pad = jnp.zeros((m.shape[0], 2 * LANES - GATE_LORA), m.dtype)
        return jnp.concatenate([r, k, v, wf, af, wb, ab, g, pad], axis=1)

    w_a = perm_pa(w_pa)
    mu_p = perm_pa(mu)

    ng = A_WIDTH // LANES
    zl = jnp.zeros((DECAY_LORA, LANES), F32)
    wup = []
    base = []
    for d in range(2):
        rows_d = []
        base_d = []
        for g in range(ng):
            cs = slice(g * LANES, (g + 1) * LANES)
            top = jnp.concatenate([decay_w2[l, d][:, cs], zl], axis=1)
            bot = jnp.concatenate([zl, iclr_a2[l, d][:, cs]], axis=1)
            rows_d.append(jnp.concatenate([top, bot], axis=0))
            base_d.append(jnp.concatenate([decay_w0[l, d][cs], iclr_a0[l, d][cs]])[None, :])
        wup.append(jnp.stack(rows_d))
        base.append(jnp.stack(base_d))
    wup = jnp.stack(wup).astype(BF16)
    base = jnp.stack(base)

    g2 = jnp.concatenate([gate_g2[l], jnp.zeros((2 * LANES - GATE_LORA, A_WIDTH), F32)], axis=0)

    wq = w_uq[l].reshape(Q_RANK, B_HEADS, QK_NOPE + QK_ROPE)
    wq_n, wq_r = wq[..., :QK_NOPE], wq[..., QK_NOPE:]
    zq = jnp.zeros((Q_RANK, B_HEADS, LANES - QK_NOPE - QK_ROPE), F32)
    wq1 = jnp.concatenate([wq_n, wq_r, zq], axis=-1).reshape(Q_RANK, B_HEADS * LANES)
    wq2 = jnp.concatenate([jnp.zeros_like(wq_n), jnp.einsum('qhr,rs->qhs', wq_r, rot), zq],
                          axis=-1).reshape(Q_RANK, B_HEADS * LANES)
    wkv = w_ukv[l].reshape(KV_RANK, B_HEADS, QK_NOPE + V_DIM)
    wk_n, wv = wkv[..., :QK_NOPE], wkv[..., QK_NOPE:]
    zk = jnp.zeros((KV_RANK, B_HEADS, LANES - QK_NOPE), F32)
    wk_p = jnp.concatenate([wk_n, zk], axis=-1).reshape(KV_RANK, B_HEADS * LANES)
    zv = jnp.zeros((KV_RANK, B_HEADS, VT_ROWS - V_DIM), F32)
    wvt = jnp.concatenate([wv, zv], axis=-1).reshape(KV_RANK, B_HEADS * VT_ROWS).T
    return dict(w_g=w_g.astype(BF16), w_a=w_a.astype(BF16), mu=mu_p, wup=wup, base=base,
                g2=g2.astype(BF16), wq1=wq1.astype(BF16), wq2=wq2.astype(BF16),
                wk=wk_p.astype(BF16), wvt=wvt.astype(BF16))


def _rope_place():
    e = np.zeros((LANES, B_HEADS * LANES), np.float32)
    for h in range(B_HEADS):
        for i in range(QK_ROPE):
            e[i, h * LANES + QK_NOPE + i] = 1.0
    return jnp.asarray(e, dtype=BF16)


def _ones_rows():
    m = np.zeros((B_HEADS, VT_ROWS, LANES), np.float32)
    m[:, V_DIM:, :] = 1.0
    return jnp.asarray(m.reshape(B_HEADS * VT_ROWS, LANES))


def _trunk(x, nseq, seq, layers, consts, *, tm, tq):
    cq, sq, ck, sk = _rope_tables(seq)
    for l, p in enumerate(layers):
        gates, mla, rkv, lora = _inproj(x, p["norm_mix_g"], p["w_g"], p["w_a"], p["mu"], seq=seq, tm=tm)
        o_f, o_b = _rwkv_scan(rkv, lora, p["wup"], p["base"], p["k_k"], p["k_a"],
                              consts["masks"], consts["bones"], nseq=nseq, seq=seq, bt=min(SCAN_BT, seq))
        q, k, vt = _mla_prep(mla, p["q_norm_g"], p["kv_norm_g"], p["wq1"], p["wq2"], p["wk"], p["wvt"],
                             consts["ones_rows"], consts["e_place"], cq, sq, ck, sk, seq=seq, tm=tm)
        att = _attention(q, k, vt, nseq=nseq, seq=seq, tq=tq)
        x = _mix(x, o_f, o_b, rkv, lora, gates, att, p["g2"], p["w_oa"], p["w_ob"], p["w_out"],
                 p["r_k"], p["gn_g"], p["gn_b"], consts["bmean"], consts["bsum"], tm=tm)
        x = _ffn(x, p["norm_ffn_g"], p["ffn_g"], p["ffn_u"], p["ffn_d"], consts["final_norm_g"],
                 tm=tm, final=(l == DEPTH - 1))
    return x


def _prepare(norm_mix_g, w_in, shift_mu, decay_w2, decay_w0, iclr_a2, iclr_a0, gate_g2,
             k_k, k_a, r_k, gn_g, gn_b, w_oa, q_norm_g, w_uq, kv_norm_g, w_ukv, w_ob,
             w_out, norm_ffn_g, w_gu, w_down, final_norm_g):
    layers = []
    for l in range(DEPTH):
        p = _prep_layer(l, w_in, shift_mu, decay_w2, decay_w0, iclr_a2, iclr_a0, gate_g2, w_uq, w_ukv)
        p.update(norm_mix_g=norm_mix_g[l][None, :], k_k=k_k[l][None, :], k_a=k_a[l][None, :],
                 r_k=r_k[l].reshape(1, A_WIDTH), gn_g=gn_g[l][None, :], gn_b=gn_b[l][None, :],
                 w_oa=w_oa[l].astype(BF16), w_ob=w_ob[l].astype(BF16), w_out=w_out[l].astype(BF16),
                 q_norm_g=q_norm_g[l][None, :], kv_norm_g=kv_norm_g[l][None, :],
                 norm_ffn_g=norm_ffn_g[l][None, :], ffn_g=w_gu[l][:, :D_FF].astype(BF16),
                 ffn_u=w_gu[l][:, D_FF:].astype(BF16), ffn_d=w_down[l].astype(BF16))
        layers.append(p)
    consts = dict(masks=_scan_masks(), bones=_block_diag_ones(LANES, A_HEAD_DIM, 1.0),
                  bmean=_block_diag_ones(A_WIDTH, A_HEAD_DIM, 1.0 / A_HEAD_DIM),
                  bsum=_block_diag_ones(A_WIDTH, A_HEAD_DIM, 1.0), e_place=_rope_place(),
                  ones_rows=_ones_rows(), final_norm_g=final_norm_g[None, :])
    return layers, consts


def kernel(x_prompt, x_sample, norm_mix_g, w_in, shift_mu, decay_w2, decay_w0, iclr_a2, iclr_a0, gate_g2, k_k, k_a, r_k, gn_g, gn_b, w_oa, q_norm_g, w_uq, kv_norm_g, w_ukv, w_ob, w_out, norm_ffn_g, w_gu, w_down, final_norm_g):
    layers, consts = _prepare(norm_mix_g, w_in, shift_mu, decay_w2, decay_w0, iclr_a2, iclr_a0, gate_g2,
                              k_k, k_a, r_k, gn_g, gn_b, w_oa, q_norm_g, w_uq, kv_norm_g, w_ukv, w_ob,
                              w_out, norm_ffn_g, w_gu, w_down, final_norm_g)
    outs = []
    for x in (x_prompt, x_sample):
        nseq, seq, _ = x.shape
        y = _trunk(x.reshape(nseq * seq, D_MODEL), nseq, seq, layers, consts,
                   tm=min(512, seq), tq=min(256, seq))
        outs.append(y.reshape(nseq, seq, D_MODEL))
    return tuple(outs)
```

```python
import functools
import math

import jax
import jax.numpy as jnp
import numpy as np
from jax import lax
from jax.experimental import pallas as pl
from jax.experimental.pallas import tpu as pltpu

F32 = jnp.float32
BF16 = jnp.bfloat16

D_MODEL = 1024
DEPTH = 2
A_HEADS = 8
A_HEAD_DIM = 64
A_WIDTH = A_HEADS * A_HEAD_DIM
DECAY_LORA = 64
ICLR_LORA = 64
GATE_LORA = 160
GN_EPS = 64e-5
B_HEADS = 8
Q_RANK = 256
KV_RANK = 128
QK_NOPE = 64
QK_ROPE = 32
V_DIM = 64
B_WIDTH = B_HEADS * V_DIM
ROPE_THETA = 10000.0
D_FF = 2816
RMS_EPS = 1e-6
SHIFT_WIDTH = 3 * A_WIDTH + 2 * DECAY_LORA + 2 * ICLR_LORA + GATE_LORA

LANES = 128
HEADS_PER_GROUP = LANES // A_HEAD_DIM
CHUNK = 64
UNIT = 2 * CHUNK
SCAN_BT = 512
HALO = 16
GATES_W = 2 * D_MODEL
MLA_W = 640
RKV_W = 3 * A_WIDTH
LORA_W = 512
PA_W = RKV_W + LORA_W
VT_ROWS = V_DIM + 16
ATTN_HEADS = 8
ATTN_KCHUNK = 512
VMEM_LIMIT = 56 * 1024 * 1024


def _dot(a, b):
    return jnp.dot(a, b, preferred_element_type=F32)


def _dot_nt(a, b):
    return lax.dot_general(a, b, (((1,), (1,)), ((), ())), preferred_element_type=F32)


def _dot_tn(a, b):
    return lax.dot_general(a, b, (((0,), (0,)), ((), ())), preferred_element_type=F32)


def _params(sem):
    return pltpu.CompilerParams(dimension_semantics=sem, vmem_limit_bytes=VMEM_LIMIT)


def _inproj_kernel(xm_ref, xp_ref, xn_ref, g_ref, wg_ref, wa_ref, mu_ref,
                   gates_ref, mla_ref, rkv_ref, lora_ref, hall_ref, pall_ref, *, tm, seq):
    i = pl.program_id(0)
    g = g_ref[...]

    def norm(x):
        ms = jnp.mean(x * x, axis=-1, keepdims=True)
        return x * lax.rsqrt(ms + RMS_EPS) * g

    t0 = i * tm
    has_prev = ((t0 % seq) != 0).astype(F32)
    has_next = (((t0 + tm) % seq) != 0).astype(F32)
    hm = norm(xm_ref[...]).astype(BF16)
    hall_ref[0:HALO, :] = (norm(xp_ref[...]) * has_prev).astype(BF16)
    hall_ref[HALO:HALO + tm, :] = hm
    hall_ref[HALO + tm:, :] = (norm(xn_ref[...]) * has_next).astype(BF16)

    ga = _dot(hm, wg_ref[...])
    gates_ref[...] = ga[:, :GATES_W].astype(BF16)
    mla_ref[...] = ga[:, GATES_W:].astype(BF16)

    ncol = 512
    for c in range(PA_W // ncol):
        cs = slice(c * ncol, (c + 1) * ncol)
        pall_ref[...] = _dot(hall_ref[...], wa_ref[:, cs])
        mu0 = mu_ref[0:1, cs]
        mu1 = mu_ref[1:2, cs]
        cur = pall_ref[HALO:HALO + tm, :]
        prev = pall_ref[HALO - 1:HALO - 1 + tm, :]
        nxt = pall_ref[HALO + 1:HALO + 1 + tm, :]
        sh = (cur + mu0 * (prev - cur) + mu1 * (nxt - cur)).astype(BF16)
        if c < RKV_W // ncol:
            rkv_ref[:, cs] = sh
        else:
            lora_ref[...] = sh


def _inproj(x, g, wg, wa, mu, *, seq, tm):
    t = x.shape[0]
    nt = t // tm
    hb = tm // HALO
    last = t // HALO - 1
    kern = functools.partial(_inproj_kernel, tm=tm, seq=seq)
    return pl.pallas_call(
        kern,
        grid=(nt,),
        in_specs=[
            pl.BlockSpec((tm, D_MODEL), lambda i: (i, 0)),
            pl.BlockSpec((HALO, D_MODEL), lambda i: (jnp.maximum(i * hb - 1, 0), 0)),
            pl.BlockSpec((HALO, D_MODEL), lambda i: (jnp.minimum((i + 1) * hb, last), 0)),
            pl.BlockSpec((1, D_MODEL), lambda i: (0, 0)),
            pl.BlockSpec((D_MODEL, GATES_W + MLA_W), lambda i: (0, 0)),
            pl.BlockSpec((D_MODEL, PA_W), lambda i: (0, 0)),
            pl.BlockSpec((2, PA_W), lambda i: (0, 0)),
        ],
        out_specs=[
            pl.BlockSpec((tm, GATES_W), lambda i: (i, 0)),
            pl.BlockSpec((tm, MLA_W), lambda i: (i, 0)),
            pl.BlockSpec((tm, RKV_W), lambda i: (i, 0)),
            pl.BlockSpec((tm, LORA_W), lambda i: (i, 0)),
        ],
        out_shape=[
            jax.ShapeDtypeStruct((t, GATES_W), BF16),
            jax.ShapeDtypeStruct((t, MLA_W), BF16),
            jax.ShapeDtypeStruct((t, RKV_W), BF16),
            jax.ShapeDtypeStruct((t, LORA_W), BF16),
        ],
        scratch_shapes=[
            pltpu.VMEM((tm + 2 * HALO, D_MODEL), BF16),
            pltpu.VMEM((tm + 2 * HALO, 512), F32),
        ],
        compiler_params=_params(("parallel",)),
        name="inproj",
    )(x, x, x, g, wg, wa, mu)


def _rwkv_kernel(rf_ref, kf_ref, vf_ref, lf_ref, rb_ref, kb_ref, vb_ref, lb_ref,
                 wup_ref, base_ref, kkw_ref, kaw_ref, msk_ref, bones_ref,
                 of_ref, ob_ref, s_ref, rp_sv, o0_sv, gx_sv, slx_sv, pc_sv, *, bt, nb):
    i = pl.program_id(0)

    @pl.when(i == 0)
    def _():
        for ref in (s_ref, rp_sv, o0_sv, gx_sv, slx_sv, pc_sv):
            ref[...] = jnp.zeros_like(ref)

    n, half = UNIT, CHUNK
    nu = bt // UNIT
    bones = bones_ref[...]
    bd = bones.astype(F32)
    kkw = kkw_ref[...]
    kaw = kaw_ref[...]
    lane = lax.broadcasted_iota(jnp.int32, (n, LANES), 1)
    lane2 = lax.broadcasted_iota(jnp.int32, (2 * n, LANES), 1)
    m_strict = [msk_ref[d, 0] for d in range(2)]
    m_incl = [msk_ref[d, 1] for d in range(2)]
    m_incl_b = [m.astype(BF16) for m in m_incl]
    zero_b = jnp.zeros((n, n), BF16)
    zrow = jnp.zeros((CHUNK, LANES), BF16)

    def tile_bd(m_rc):
        return (jnp.concatenate([m_rc, m_rc], axis=0) * bd).astype(BF16)

    def pair_bd(m0, m1):
        return jnp.concatenate([jnp.concatenate([m0, zero_b], axis=1),
                                jnp.concatenate([zero_b, m1], axis=1)], axis=0)

    def tile_bd2(m_rc2):
        return pair_bd(tile_bd(m_rc2[:, :LANES]), tile_bd(m_rc2[:, LANES:]))

    def head_split(x):
        lane_x = lax.broadcasted_iota(jnp.int32, x.shape, 1)
        in_h0 = (lane_x & (LANES - 1)) < A_HEAD_DIM
        return jnp.concatenate([jnp.where(in_h0, x, 0.0), jnp.where(in_h0, 0.0, x)], axis=0).astype(BF16)

    def by_chunk(m):
        return jnp.concatenate([jnp.concatenate([m[:CHUNK], zrow], axis=0),
                                jnp.concatenate([zrow, m[CHUNK:]], axis=0)], axis=1)

    def st_preact(c):
        r_ref, k_ref, v_ref, l_ref, _ = c["refs"]
        rows, d = c["rows"], c["d"]
        lo = l_ref[rows, :].astype(F32)
        lo_t = jnp.where(lane < DECAY_LORA, jnp.tanh(lo), lo).astype(BF16)
        c["up"] = _dot(lo_t, wup_ref[d]) + base_ref[d]
        c["r"] = r_ref[rows, :].astype(F32)
        c["pk"] = k_ref[rows, :].astype(F32)
        c["v"] = v_ref[rows, :]
        c["kkraw"] = c["pk"] * kkw
        c["ss"] = _dot((c["kkraw"] * c["kkraw"]).astype(BF16), bones)

    def st_decay(c):
        d, up = c["d"], c["up"]
        sg = jax.nn.sigmoid(up)
        lw = -math.exp(-0.5) * sg[:, :LANES]
        c["a"] = sg[:, LANES:]
        lw_hi = lw.astype(BF16)
        lw_lo = (lw - lw_hi.astype(F32)).astype(BF16)
        ci2 = _dot(m_incl_b[d], jnp.concatenate([lw_hi, lw_lo], axis=1))
        ci = ci2[:, :LANES] + ci2[:, LANES:]
        ends = [ck * CHUNK + (CHUNK - 1 if d == 0 else 0) for ck in range(UNIT // CHUNK)]
        c["ctot"] = [ci[e:e + 1, :] for e in ends]
        c["lw"], c["ci"] = lw, ci
        c["crest"] = jnp.concatenate([jnp.broadcast_to(t, (CHUNK, LANES)) for t in c["ctot"]], axis=0) - ci

    def st_interact(c):
        a, pk, ci, crest = c["a"], c["pk"], c["ci"], c["crest"]
        kk = c["kkraw"] * lax.rsqrt(jnp.maximum(c["ss"], 1e-24))
        kd = pk * (1.0 + (a - 1.0) * kaw)
        b = kk * a
        at = -kk * jnp.exp(ci - c["lw"])
        rt = c["r"] * jnp.exp(ci)
        e_m = jnp.exp(-ci)
        e_r = jnp.exp(crest)
        c["at"], c["rt"] = at, rt
        c["bh"] = (b * e_r).astype(BF16)
        c["kh"] = (kd * e_r).astype(BF16)
        lhs = jnp.concatenate([at, rt], axis=0)
        rhs = jnp.concatenate([b * e_m, kd * e_m], axis=0).astype(BF16)
        lhs2 = jnp.concatenate([jnp.where(lane2 < A_HEAD_DIM, lhs, 0.0),
                                jnp.where(lane2 < A_HEAD_DIM, 0.0, lhs)], axis=0).astype(BF16)
        c["g"] = _dot_nt(lhs2, rhs)

    def st_masks(c):
        d, g2 = c["d"], c["g"]
        blk = lambda h, i, k: g2[(2 * h + i) * n:(2 * h + i + 1) * n, k * n:(k + 1) * n]
        a_bd = [blk(h, 0, 0) * m_strict[d] for h in range(2)]
        c["a_akrk"] = jnp.concatenate(
            [jnp.concatenate([blk(h, 0, 1) * m_strict[d] for h in range(2)], axis=1),
             jnp.concatenate([blk(h, 1, 1) * m_incl[d] for h in range(2)], axis=1)], axis=0).astype(BF16)
        c["a_rb"] = jnp.concatenate([blk(h, 1, 0) * m_incl[d] for h in range(2)], axis=1).astype(BF16)
        a_rc = jnp.concatenate([a[:half] + a[half:] for a in a_bd], axis=1)
        c["nm"] = a_rc
        c["p"] = _dot(a_rc.astype(BF16), pair_bd(a_bd[0].astype(BF16), a_bd[1].astype(BF16)))

    def st_level(c):
        nm, p = c["nm"], c["p"]
        tp = _dot(jnp.concatenate([nm, p], axis=0).astype(BF16), tile_bd2(p))
        c["nm"] = nm + p + tp[:half]
        c["p"] = tp[half:]

    def st_last_level(c):
        nm, p = c["nm"], c["p"]
        nm = nm + p + _dot(nm.astype(BF16), tile_bd2(p))
        c["nm_bd"] = jnp.concatenate([tile_bd(nm[:, :LANES]), tile_bd(nm[:, LANES:])], axis=1)

    def st_wo(c):
        wo = _dot(c["a_akrk"], head_split(c["v"].astype(F32)))
        c["w"], c["o0b"] = wo[:n], wo[n:]

    def st_tx(c):
        x = jnp.concatenate([c["at"], c["w"]], axis=1)
        c["tx"] = x + _dot(c["nm_bd"], head_split(x))

    def st_rx(c):
        rx = _dot(c["a_rb"], head_split(c["tx"]))
        c["o0"] = rx[:, LANES:] + c["o0b"]
        c["apu"] = c["tx"].astype(BF16)
        c["rp"] = (c["rt"] + rx[:, :LANES]).astype(BF16)

    def st_trans(c):
        gu = _dot_tn(c["apu"], by_chunk(c["bh"]))
        vk = _dot_tn(c["v"], by_chunk(c["kh"]))
        c["gx"] = [(gu[:LANES, ck * LANES:(ck + 1) * LANES] * bd).astype(BF16) for ck in range(2)]
        c["slx"] = [(gu[LANES:, ck * LANES:(ck + 1) * LANES] + vk[:, ck * LANES:(ck + 1) * LANES]) * bd
                    for ck in range(2)]
        c["pc"] = [jnp.exp(t) for t in c["ctot"]]

    nck = UNIT // CHUNK
    groups = []
    for d in range(2):
        refs = ((rf_ref, kf_ref, vf_ref, lf_ref, of_ref) if d == 0
                else (rb_ref, kb_ref, vb_ref, lb_ref, ob_ref))
        groups.append([dict(d=d, u=u, rows=slice(u * UNIT, (u + 1) * UNIT), refs=refs)
                       for u in (range(nu) if d == 0 else range(nu - 1, -1, -1))])

    stage_list = [st_preact, st_decay, st_interact, st_masks] + [st_level] * 4 + [
        st_last_level, st_wo, st_tx, st_rx, st_trans]
    work = [(st, c) for st in stage_list for grp in groups for c in grp]
    first = lax.rem(jnp.maximum(i - 1, 0), nb) == 0
    s = [jnp.where(first, 0.0, s_ref[d]) for d in range(2)]
    seqs = [[(c, ck) for c in groups[d] for ck in (range(nck) if d == 0 else range(nck - 1, -1, -1))]
            for d in range(2)]
    rounds = len(seqs[0])
    per_round = -(-len(work) // rounds)
    outs = {}
    for q in range(rounds):
        for d in range(2):
            c, ck = seqs[d][q]
            u = c["u"]
            sl = slice(u * UNIT + ck * CHUNK, u * UNIT + (ck + 1) * CHUNK)
            s_b = s[d].astype(BF16)
            outs[(d, u, ck)] = _dot_nt(rp_sv[d, sl, :], s_b) + o0_sv[d, sl, :]
            s[d] = s[d] * pc_sv[d, u, ck, 0:1, :] + _dot(s_b, gx_sv[d, u, ck]) + slx_sv[d, u, ck]
        for st, c in work[q * per_round:(q + 1) * per_round]:
            st(c)
    for grp in groups:
        for c in grp:
            c["refs"][4][c["rows"], :] = jnp.concatenate(
                [outs[(c["d"], c["u"], ck)] for ck in range(nck)], axis=0)
    s_ref[0] = s[0]
    s_ref[1] = s[1]
    for grp in groups:
        for c in grp:
            d, u = c["d"], c["u"]
            rp_sv[d, c["rows"], :] = c["rp"]
            o0_sv[d, c["rows"], :] = c["o0"]
            for ck in range(nck):
                gx_sv[d, u, ck] = c["gx"][ck]
                slx_sv[d, u, ck] = c["slx"][ck]
                pc_sv[d, u, ck] = jnp.broadcast_to(c["pc"][ck], (8, LANES))


def _rwkv_scan(rkv, lora, wup, base, kkw, kaw, masks, bones, *, nseq, seq, bt):
    t = rkv.shape[0]
    nb = seq // bt
    ng = A_WIDTH // LANES
    nsteps = nseq * ng * nb
    nu = bt // UNIT
    nck = UNIT // CHUNK

    def decode(i):
        return i // (ng * nb), (i // nb) % ng, i % nb

    def cur(i):
        return decode(jnp.minimum(i, nsteps - 1))

    def prev(i):
        return decode(jnp.maximum(i - 1, 0))

    def fw(bgj):
        b, g, j = bgj
        return b * nb + j

    def bw(bgj):
        b, g, j = bgj
        return b * nb + nb - 1 - j

    def tok_specs(tokmap, dcol):
        return [
            pl.BlockSpec((bt, LANES), lambda i: (tokmap(cur(i)), cur(i)[1])),
            pl.BlockSpec((bt, LANES), lambda i: (tokmap(cur(i)), ng + cur(i)[1])),
            pl.BlockSpec((bt, LANES), lambda i: (tokmap(cur(i)), 2 * ng + cur(i)[1])),
            pl.BlockSpec((bt, LANES), lambda i: (tokmap(cur(i)), dcol)),
        ]

    kern = functools.partial(_rwkv_kernel, bt=bt, nb=nb)
    return pl.pallas_call(
        kern,
        grid=(nsteps + 1,),
        in_specs=tok_specs(fw, 0) + tok_specs(bw, 1) + [
            pl.BlockSpec((2, None, LANES, 2 * LANES), lambda i: (0, cur(i)[1], 0, 0)),
            pl.BlockSpec((2, None, 1, 2 * LANES), lambda i: (0, cur(i)[1], 0, 0)),
            pl.BlockSpec((1, LANES), lambda i: (0, cur(i)[1])),
            pl.BlockSpec((1, LANES), lambda i: (0, cur(i)[1])),
            pl.BlockSpec((2, 2, UNIT, UNIT), lambda i: (0, 0, 0, 0)),
            pl.BlockSpec((LANES, LANES), lambda i: (0, 0)),
        ],
        out_specs=[
            pl.BlockSpec((bt, LANES), lambda i: (fw(prev(i)), prev(i)[1])),
            pl.BlockSpec((bt, LANES), lambda i: (bw(prev(i)), prev(i)[1])),
        ],
        out_shape=[jax.ShapeDtypeStruct((t, A_WIDTH), F32), jax.ShapeDtypeStruct((t, A_WIDTH), F32)],
        scratch_shapes=[
            pltpu.VMEM((2, LANES, LANES), F32),
            pltpu.VMEM((2, bt, LANES), BF16),
            pltpu.VMEM((2, bt, LANES), F32),
            pltpu.VMEM((2, nu, nck, LANES, LANES), BF16),
            pltpu.VMEM((2, nu, nck, LANES, LANES), F32),
            pltpu.VMEM((2, nu, nck, 8, LANES), F32),
        ],
        compiler_params=_params(("arbitrary",)),
        name="rwkv_scan",
    )(rkv, rkv, rkv, lora, rkv, rkv, rkv, lora, wup, base, kkw, kaw, masks, bones)


def _mla_prep_kernel(mla_ref, qg_ref, kvg_ref, wq1_ref, wq2_ref, wk_ref, wvt_ref, ones_ref, e_ref,
                     cq_ref, sq_ref, ck_ref, sk_ref, q_ref, k_ref, vt_ref):
    tm = mla_ref.shape[0]

    def norm(x, g):
        ms = jnp.mean(x * x, axis=-1, keepdims=True)
        return x * lax.rsqrt(ms + RMS_EPS) * g

    pq = mla_ref[:, 0:Q_RANK].astype(F32)
    pkv = mla_ref[:, Q_RANK:Q_RANK + KV_RANK].astype(F32)
    pkr = mla_ref[:, Q_RANK + KV_RANK:Q_RANK + KV_RANK + LANES].astype(F32)
    pkrot = mla_ref[:, Q_RANK + KV_RANK + LANES:].astype(F32)

    hq = norm(pq, qg_ref[...]).astype(BF16)
    q1 = _dot(hq, wq1_ref[...])
    q2 = _dot(hq, wq2_ref[...])
    cq = jnp.concatenate([cq_ref[...]] * B_HEADS, axis=1)
    sq = jnp.concatenate([sq_ref[...]] * B_HEADS, axis=1)
    scale = (QK_NOPE + QK_ROPE) ** -0.5 * math.log2(math.e)
    q_ref[...] = ((q1 * cq + q2 * sq) * scale).astype(BF16)

    hkv = norm(pkv, kvg_ref[...]).astype(BF16)
    krope = (pkr * ck_ref[...] + pkrot * sk_ref[...]).astype(BF16)
    k_ref[...] = (_dot(hkv, wk_ref[...]) + _dot(krope, e_ref[...])).astype(BF16)
    ones = jnp.concatenate([ones_ref[...]] * (tm // LANES), axis=1)
    vt_ref[...] = (_dot_nt(wvt_ref[...], hkv) + ones).astype(BF16)


def _mla_prep(mla, qg, kvg, wq1, wq2, wk, wvt, ones, e, cq, sq, ck, sk, *, seq, tm):
    t = mla.shape[0]
    nt = t // tm
    per = seq // tm
    kw = B_HEADS * LANES
    vr = B_HEADS * VT_ROWS
    tab = pl.BlockSpec((tm, LANES), lambda i: (i % per, 0))
    return pl.pallas_call(
        _mla_prep_kernel,
        grid=(nt,),
        in_specs=[
            pl.BlockSpec((tm, MLA_W), lambda i: (i, 0)),
            pl.BlockSpec((1, Q_RANK), lambda i: (0, 0)),
            pl.BlockSpec((1, KV_RANK), lambda i: (0, 0)),
            pl.BlockSpec((Q_RANK, kw), lambda i: (0, 0)),
            pl.BlockSpec((Q_RANK, kw), lambda i: (0, 0)),
            pl.BlockSpec((KV_RANK, kw), lambda i: (0, 0)),
            pl.BlockSpec((vr, KV_RANK), lambda i: (0, 0)),
            pl.BlockSpec((vr, LANES), lambda i: (0, 0)),
            pl.BlockSpec((LANES, kw), lambda i: (0, 0)),
            tab, tab, tab, tab,
        ],
        out_specs=[
            pl.BlockSpec((tm, kw), lambda i: (i, 0)),
            pl.BlockSpec((tm, kw), lambda i: (i, 0)),
            pl.BlockSpec((vr, tm), lambda i: (0, i)),
        ],
        out_shape=[
            jax.ShapeDtypeStruct((t, kw), BF16),
            jax.ShapeDtypeStruct((t, kw), BF16),
            jax.ShapeDtypeStruct((vr, t), BF16),
        ],
        compiler_params=_params(("parallel",)),
        name="mla_prep",
    )(mla, qg, kvg, wq1, wq2, wk, wvt, ones, e, cq, sq, ck, sk)


def _attn_kernel(q_ref, k_ref, vt_ref, o_ref):
    nh = ATTN_HEADS
    tq = q_ref.shape[0]
    kc = min(ATTN_KCHUNK, k_ref.shape[0])
    nc = k_ref.shape[0] // kc

    def scores(h, i):
        s = _dot_nt(k_ref[i * kc:(i + 1) * kc, h * LANES:(h + 1) * LANES],
                    q_ref[:, h * LANES:(h + 1) * LANES])
        return s, jnp.max(s.reshape(kc // 8, 8, tq), axis=0)

    outs = []
    cur = [scores(0, i) for i in range(nc)]
    for h in range(nh):
        m8 = cur[0][1]
        for _, part in cur[1:]:
            m8 = jnp.maximum(m8, part)
        m = jnp.max(m8, axis=0, keepdims=True)
        nxt = []
        ot = None
        for i in range(nc):
            if h + 1 < nh:
                nxt.append(scores(h + 1, i))
            p = jnp.exp2((cur[i][0] - m).astype(BF16))
            part = _dot(vt_ref[h * VT_ROWS:(h + 1) * VT_ROWS, i * kc:(i + 1) * kc], p)
            ot = part if ot is None else ot + part
        outs.append(ot[:V_DIM] / ot[V_DIM:V_DIM + 1])
        cur = nxt
    for g in range(nh // 2):
        o_ref[:, g * LANES:(g + 1) * LANES] = jnp.concatenate(outs[2 * g:2 * g + 2], axis=0).T.astype(BF16)


def _attention(q, k, vt, *, nseq, seq, tq):
    t = q.shape[0]
    nq = seq // tq
    nh = ATTN_HEADS
    ngrp = B_HEADS // nh
    return pl.pallas_call(
        _attn_kernel,
        grid=(nseq, ngrp, nq),
        in_specs=[
            pl.BlockSpec((tq, nh * LANES), lambda b, g, i: (b * nq + i, g)),
            pl.BlockSpec((seq, nh * LANES), lambda b, g, i: (b, g)),
            pl.BlockSpec((nh * VT_ROWS, seq), lambda b, g, i: (g, b)),
        ],
        out_specs=pl.BlockSpec((tq, nh * V_DIM), lambda b, g, i: (b * nq + i, g)),
        out_shape=jax.ShapeDtypeStruct((t, B_WIDTH), BF16),
        compiler_params=_params(("parallel", "parallel", "arbitrary")),
        name="mla_attention",
    )(q, k, vt)


def _mix_kernel(x_ref, of_ref, ob_ref, r_ref, k_ref, v_ref, gl_ref, gates_ref, att_ref,
                g2_ref, woa_ref, wob_ref, wout_ref, rk_ref, gng_ref, gnb_ref,
                bmean_ref, bsum_ref, o_ref):
    half = A_WIDTH // 2

    def head_stat(x, w):
        xb = x.astype(BF16)
        return jnp.concatenate([_dot(xb[:, :half], w), _dot(xb[:, half:], w)], axis=1)

    o = of_ref[...] + ob_ref[...]
    bmean = bmean_ref[...]
    mean = head_stat(o, bmean)
    dc = o - mean
    var = head_stat(dc * dc, bmean)
    on = dc * lax.rsqrt(var + GN_EPS) * gng_ref[...] + gnb_ref[...]
    r = r_ref[...].astype(F32)
    pk = k_ref[...].astype(F32)
    v = v_ref[...].astype(F32)
    bonus = head_stat(r * pk * rk_ref[...], bsum_ref[...]) * v
    g = _dot(jax.nn.sigmoid(gl_ref[...].astype(F32)).astype(BF16), g2_ref[...])
    ya = _dot(((on + bonus) * g).astype(BF16), woa_ref[...])
    yb = _dot(att_ref[...], wob_ref[...])
    sg = jax.nn.sigmoid(gates_ref[...].astype(F32))
    mix = (sg[:, :D_MODEL] * ya + sg[:, D_MODEL:] * yb).astype(BF16)
    o_ref[...] = x_ref[...] + _dot(mix, wout_ref[...])


def _mix(x, o_f, o_b, rkv, lora, gates, att, g2, woa, wob, wout, rk, gng, gnb, bmean, bsum, *, tm):
    t = x.shape[0]
    nt = t // tm

    def const(shape):
        return pl.BlockSpec(shape, lambda i: tuple(0 for _ in shape))

    return pl.pallas_call(
        _mix_kernel,
        grid=(nt,),
        in_specs=[
            pl.BlockSpec((tm, D_MODEL), lambda i: (i, 0)),
            pl.BlockSpec((tm, A_WIDTH), lambda i: (i, 0)),
            pl.BlockSpec((tm, A_WIDTH), lambda i: (i, 0)),
            pl.BlockSpec((tm, A_WIDTH), lambda i: (i, 0)),
            pl.BlockSpec((tm, A_WIDTH), lambda i: (i, 1)),
            pl.BlockSpec((tm, A_WIDTH), lambda i: (i, 2)),
            pl.BlockSpec((tm, 2 * LANES), lambda i: (i, 1)),
            pl.BlockSpec((tm, GATES_W), lambda i: (i, 0)),
            pl.BlockSpec((tm, B_WIDTH), lambda i: (i, 0)),
            const((2 * LANES, A_WIDTH)),
            const((A_WIDTH, D_MODEL)),
            const((B_WIDTH, D_MODEL)),
            const((D_MODEL, D_MODEL)),
            const((1, A_WIDTH)),
            const((1, A_WIDTH)),
            const((1, A_WIDTH)),
            const((A_WIDTH // 2, A_WIDTH // 2)),
            const((A_WIDTH // 2, A_WIDTH // 2)),
        ],
        out_specs=pl.BlockSpec((tm, D_MODEL), lambda i: (i, 0)),
        out_shape=jax.ShapeDtypeStruct((t, D_MODEL), F32),
        compiler_params=_params(("parallel",)),
        name="mix",
    )(x, o_f, o_b, rkv, rkv, rkv, lora, gates, att, g2, woa, wob, wout, rk, gng, gnb, bmean, bsum)


def _ffn_kernel(x_ref, g_ref, wg_ref, wu_ref, wd_ref, fg_ref, o_ref, *, final, fc):
    x = x_ref[...]
    ms = jnp.mean(x * x, axis=-1, keepdims=True)
    h = (x * lax.rsqrt(ms + RMS_EPS) * g_ref[...]).astype(BF16)
    acc = x
    for c in range(D_FF // fc):
        cs = slice(c * fc, (c + 1) * fc)
        gt = _dot(h, wg_ref[:, cs])
        up = _dot(h, wu_ref[:, cs])
        act = (gt * jax.nn.sigmoid(gt) * up).astype(BF16)
        acc = acc + _dot(act, wd_ref[cs, :])
    if final:
        ms2 = jnp.mean(acc * acc, axis=-1, keepdims=True)
        acc = acc * lax.rsqrt(ms2 + RMS_EPS) * fg_ref[...]
    o_ref[...] = acc


def _ffn(x, g, wg, wu, wd, fg, *, tm, final):
    t = x.shape[0]
    nt = t // tm
    kern = functools.partial(_ffn_kernel, final=final, fc=D_FF)
    return pl.pallas_call(
        kern,
        grid=(nt,),
        in_specs=[
            pl.BlockSpec((tm, D_MODEL), lambda i: (i, 0)),
            pl.BlockSpec((1, D_MODEL), lambda i: (0, 0)),
            pl.BlockSpec((D_MODEL, D_FF), lambda i: (0, 0)),
            pl.BlockSpec((D_MODEL, D_FF), lambda i: (0, 0)),
            pl.BlockSpec((D_FF, D_MODEL), lambda i: (0, 0)),
            pl.BlockSpec((1, D_MODEL), lambda i: (0, 0)),
        ],
        out_specs=pl.BlockSpec((tm, D_MODEL), lambda i: (i, 0)),
        out_shape=jax.ShapeDtypeStruct((t, D_MODEL), F32),
        compiler_params=_params(("parallel",)),
        name="ffn",
    )(x, g, wg, wu, wd, fg)


def _rot_matrix():
    half = QK_ROPE // 2
    m = np.zeros((QK_ROPE, QK_ROPE), np.float32)
    for i in range(half):
        m[i + half, i] = -1.0
        m[i, i + half] = 1.0
    return jnp.asarray(m)


def _scan_masks():
    idx = np.arange(UNIT)
    row, col = idx[:, None], idx[None, :]
    same = (row // CHUNK) == (col // CHUNK)
    out = np.zeros((2, 2, UNIT, UNIT), np.float32)
    for d, before in enumerate((col < row, col > row)):
        out[d, 0] = same & before
        out[d, 1] = same & (before | (row == col))
    return jnp.asarray(out)


def _block_diag_ones(width, block, value):
    idx = np.arange(width)
    m = ((idx[:, None] // block) == (idx[None, :] // block)).astype(np.float32) * value
    return jnp.asarray(m, dtype=BF16)


def _rope_tables(seq):
    pos = jnp.arange(seq, dtype=F32)
    inv_freq = 1.0 / (ROPE_THETA ** (jnp.arange(0, QK_ROPE, 2, dtype=F32) / QK_ROPE))
    ang = pos[:, None] * inv_freq[None, :]
    ang = jnp.concatenate([ang, ang], axis=-1)
    cos, sin = jnp.cos(ang), jnp.sin(ang)
    z = lambda w: jnp.zeros((seq, w), F32)
    cq = jnp.concatenate([jnp.ones((seq, QK_NOPE), F32), cos, z(LANES - QK_NOPE - QK_ROPE)], axis=1)
    sq = jnp.concatenate([z(QK_NOPE), sin, z(LANES - QK_NOPE - QK_ROPE)], axis=1)
    ck = jnp.concatenate([cos, z(LANES - QK_ROPE)], axis=1)
    sk = jnp.concatenate([sin, z(LANES - QK_ROPE)], axis=1)
    return cq, sq, ck, sk


def _prep_layer(l, w_in, shift_mu, decay_w2, decay_w0, iclr_a2, iclr_a0, gate_g2, w_uq, w_ukv):
    rot = _rot_matrix()
    w = w_in[l]
    off = 2 * D_MODEL
    w_gates = w[:, :off]
    w_pa = w[:, off:off + SHIFT_WIDTH]
    mu = shift_mu[l]
    o2 = off + SHIFT_WIDTH
    w_pq = w[:, o2:o2 + Q_RANK]
    w_pkv = w[:, o2 + Q_RANK:o2 + Q_RANK + KV_RANK]
    w_pkr = w[:, o2 + Q_RANK + KV_RANK:]
    zpad = jnp.zeros((D_MODEL, LANES - QK_ROPE), F32)
    w_g = jnp.concatenate([w_gates, w_pq, w_pkv, w_pkr, zpad, w_pkr @ rot, zpad], axis=1)

    def perm_pa(m):
        cuts = np.cumsum([A_WIDTH, A_WIDTH, A_WIDTH, DECAY_LORA, DECAY_LORA, ICLR_LORA, ICLR_LORA]).tolist()
        r, k, v, wf, wb, af, ab, g = jnp.split(m, cuts, axis=1)
        pad = jnp.zeros((m.shape[0], 2 * LANES - GATE_LORA), m.dtype)
        return jnp.concatenate([r, k, v, wf, af, wb, ab, g, pad], axis=1)

    w_a = perm_pa(w_pa)
    mu_p = perm_pa(mu)

    ng = A_WIDTH // LANES
    zl = jnp.zeros((DECAY_LORA, LANES), F32)
    wup = []
    base = []
    for d in range(2):
        rows_d = []
        base_d = []
        for g in range(ng):
            cs = slice(g * LANES, (g + 1) * LANES)
            top = jnp.concatenate([decay_w2[l, d][:, cs], zl], axis=1)
            bot = jnp.concatenate([zl, iclr_a2[l, d][:, cs]], axis=1)
            rows_d.append(jnp.concatenate([top, bot], axis=0))
            base_d.append(jnp.concatenate([decay_w0[l, d][cs], iclr_a0[l, d][cs]])[None, :])
        wup.append(jnp.stack(rows_d))
        base.append(jnp.stack(base_d))
    wup = jnp.stack(wup).astype(BF16)
    base = jnp.stack(base)

    g2 = jnp.concatenate([gate_g2[l], jnp.zeros((2 * LANES - GATE_LORA, A_WIDTH), F32)], axis=0)

    wq = w_uq[l].reshape(Q_RANK, B_HEADS, QK_NOPE + QK_ROPE)
    wq_n, wq_r = wq[..., :QK_NOPE], wq[..., QK_NOPE:]
    zq = jnp.zeros((Q_RANK, B_HEADS, LANES - QK_NOPE - QK_ROPE), F32)
    wq1 = jnp.concatenate([wq_n, wq_r, zq], axis=-1).reshape(Q_RANK, B_HEADS * LANES)
    wq2 = jnp.concatenate([jnp.zeros_like(wq_n), jnp.einsum('qhr,rs->qhs', wq_r, rot), zq],
                          axis=-1).reshape(Q_RANK, B_HEADS * LANES)
    wkv = w_ukv[l].reshape(KV_RANK, B_HEADS, QK_NOPE + V_DIM)
    wk_n, wv = wkv[..., :QK_NOPE], wkv[..., QK_NOPE:]
    zk = jnp.zeros((KV_RANK, B_HEADS, LANES - QK_NOPE), F32)
    wk_p = jnp.concatenate([wk_n, zk], axis=-1).reshape(KV_RANK, B_HEADS * LANES)
    zv = jnp.zeros((KV_RANK, B_HEADS, VT_ROWS - V_DIM), F32)
    wvt = jnp.concatenate([wv, zv], axis=-1).reshape(KV_RANK, B_HEADS * VT_ROWS).T
    return dict(w_g=w_g.astype(BF16), w_a=w_a.astype(BF16), mu=mu_p, wup=wup, base=base,
                g2=g2.astype(BF16), wq1=wq1.astype(BF16), wq2=wq2.astype(BF16),
                wk=wk_p.astype(BF16), wvt=wvt.astype(BF16))


def _rope_place():
    e = np.zeros((LANES, B_HEADS * LANES), np.float32)
    for h in range(B_HEADS):
        for i in range(QK_ROPE):
            e[i, h * LANES + QK_NOPE + i] = 1.0
    return jnp.asarray(e, dtype=BF16)


def _ones_rows():
    m = np.zeros((B_HEADS, VT_ROWS, LANES), np.float32)
    m[:, V_DIM:, :] = 1.0
    return jnp.asarray(m.reshape(B_HEADS * VT_ROWS, LANES))


def _trunk(x, nseq, seq, layers, consts, *, tm, tq):
    cq, sq, ck, sk = _rope_tables(seq)
    for l, p in enumerate(layers):
        gates, mla, rkv, lora = _inproj(x, p["norm_mix_g"], p["w_g"], p["w_a"], p["mu"], seq=seq, tm=tm)
        o_f, o_b = _rwkv_scan(rkv, lora, p["wup"], p["base"], p["k_k"], p["k_a"],
                              consts["masks"], consts["bones"], nseq=nseq, seq=seq, bt=min(SCAN_BT, seq))
        q, k, vt = _mla_prep(mla, p["q_norm_g"], p["kv_norm_g"], p["wq1"], p["wq2"], p["wk"], p["wvt"],
                             consts["ones_rows"], consts["e_place"], cq, sq, ck, sk, seq=seq, tm=tm)
        att = _attention(q, k, vt, nseq=nseq, seq=seq, tq=tq)
        x = _mix(x, o_f, o_b, rkv, lora, gates, att, p["g2"], p["w_oa"], p["w_ob"], p["w_out"],
                 p["r_k"], p["gn_g"], p["gn_b"], consts["bmean"], consts["bsum"], tm=tm)
        x = _ffn(x, p["norm_ffn_g"], p["ffn_g"], p["ffn_u"], p["ffn_d"], consts["final_norm_g"],
                 tm=tm, final=(l == DEPTH - 1))
    return x


def _prepare(norm_mix_g, w_in, shift_mu, decay_w2, decay_w0, iclr_a2, iclr_a0, gate_g2,
             k_k, k_a, r_k, gn_g, gn_b, w_oa, q_norm_g, w_uq, kv_norm_g, w_ukv, w_ob,
             w_out, norm_ffn_g, w_gu, w_down, final_norm_g):
    layers = []
    for l in range(DEPTH):
        p = _prep_layer(l, w_in, shift_mu, decay_w2, decay_w0, iclr_a2, iclr_a0, gate_g2, w_uq, w_ukv)
        p.update(norm_mix_g=norm_mix_g[l][None, :], k_k=k_k[l][None, :], k_a=k_a[l][None, :],
                 r_k=r_k[l].reshape(1, A_WIDTH), gn_g=gn_g[l][None, :], gn_b=gn_b[l][None, :],
                 w_oa=w_oa[l].astype(BF16), w_ob=w_ob[l].astype(BF16), w_out=w_out[l].astype(BF16),
                 q_norm_g=q_norm_g[l][None, :], kv_norm_g=kv_norm_g[l][None, :],
                 norm_ffn_g=norm_ffn_g[l][None, :], ffn_g=w_gu[l][:, :D_FF].astype(BF16),
                 ffn_u=w_gu[l][:, D_FF:].astype(BF16), ffn_d=w_down[l].astype(BF16))
        layers.append(p)
    consts = dict(masks=_scan_masks(), bones=_block_diag_ones(LANES, A_HEAD_DIM, 1.0),
                  bmean=_block_diag_ones(A_WIDTH // 2, A_HEAD_DIM, 1.0 / A_HEAD_DIM),
                  bsum=_block_diag_ones(A_WIDTH // 2, A_HEAD_DIM, 1.0), e_place=_rope_place(),
                  ones_rows=_ones_rows(), final_norm_g=final_norm_g[None, :])
    return layers, consts


def kernel(x_prompt, x_sample, norm_mix_g, w_in, shift_mu, decay_w2, decay_w0, iclr_a2, iclr_a0, gate_g2, k_k, k_a, r_k, gn_g, gn_b, w_oa, q_norm_g, w_uq, kv_norm_g, w_ukv, w_ob, w_out, norm_ffn_g, w_gu, w_down, final_norm_g):
    layers, consts = _prepare(norm_mix_g, w_in, shift_mu, decay_w2, decay_w0, iclr_a2, iclr_a0, gate_g2,
                              k_k, k_a, r_k, gn_g, gn_b, w_oa, q_norm_g, w_uq, kv_norm_g, w_ukv, w_ob,
                              w_out, norm_ffn_g, w_gu, w_down, final_norm_g)
    outs = []
    for x in (x_prompt, x_sample):
        nseq, seq, _ = x.shape
        y = _trunk(x.reshape(nseq * seq, D_MODEL), nseq, seq, layers, consts,
                   tm=min(512, seq), tq=min(256, seq))
        outs.append(y.reshape(nseq, seq, D_MODEL))
    return tuple(outs)
```

```python
import functools
import math

import jax
import jax.numpy as jnp
import numpy as np
from jax import lax
from jax.experimental import pallas as pl
from jax.experimental.pallas import tpu as pltpu

F32 = jnp.float32
BF16 = jnp.bfloat16

D_MODEL = 1024
DEPTH = 2
A_HEADS = 8
A_HEAD_DIM = 64
A_WIDTH = A_HEADS * A_HEAD_DIM
DECAY_LORA = 64
ICLR_LORA = 64
GATE_LORA = 160
GN_EPS = 64e-5
B_HEADS = 8
Q_RANK = 256
KV_RANK = 128
QK_NOPE = 64
QK_ROPE = 32
V_DIM = 64
B_WIDTH = B_HEADS * V_DIM
ROPE_THETA = 10000.0
D_FF = 2816
RMS_EPS = 1e-6
SHIFT_WIDTH = 3 * A_WIDTH + 2 * DECAY_LORA + 2 * ICLR_LORA + GATE_LORA

LANES = 128
HEADS_PER_GROUP = LANES // A_HEAD_DIM
CHUNK = 64
UNIT = 2 * CHUNK
SCAN_BT = 512
HALO = 16
GATES_W = 2 * D_MODEL
MLA_W = 640
RKV_W = 3 * A_WIDTH
LORA_W = 512
PA_W = RKV_W + LORA_W
VT_ROWS = V_DIM + 16
ATTN_HEADS = 8
ATTN_KCHUNK = 512
ATTN_TQ = 256
VMEM_LIMIT = 56 * 1024 * 1024


def _dot(a, b):
    return jnp.dot(a, b, preferred_element_type=F32)


def _dot_nt(a, b):
    return lax.dot_general(a, b, (((1,), (1,)), ((), ())), preferred_element_type=F32)


def _dot_tn(a, b):
    return lax.dot_general(a, b, (((0,), (0,)), ((), ())), preferred_element_type=F32)


def _params(sem):
    return pltpu.CompilerParams(dimension_semantics=sem, vmem_limit_bytes=VMEM_LIMIT)


def _inproj_kernel(xm_ref, xp_ref, xn_ref, g_ref, wg_ref, wa_ref, mu_ref,
                   gates_ref, mla_ref, rkv_ref, lora_ref, hall_ref, pall_ref, *, tm, seq):
    i = pl.program_id(0)
    g = g_ref[...]

    def norm(x):
        ms = jnp.mean(x * x, axis=-1, keepdims=True)
        return x * lax.rsqrt(ms + RMS_EPS) * g

    t0 = i * tm
    has_prev = ((t0 % seq) != 0).astype(F32)
    has_next = (((t0 + tm) % seq) != 0).astype(F32)
    hm = norm(xm_ref[...]).astype(BF16)
    hall_ref[0:HALO, :] = (norm(xp_ref[...]) * has_prev).astype(BF16)
    hall_ref[HALO:HALO + tm, :] = hm
    hall_ref[HALO + tm:, :] = (norm(xn_ref[...]) * has_next).astype(BF16)

    ncol = 512
    for c in range(PA_W // ncol):
        cs = slice(c * ncol, (c + 1) * ncol)
        pall_ref[...] = _dot(hall_ref[...], wa_ref[:, cs])
        mu0 = mu_ref[0:1, cs]
        mu1 = mu_ref[1:2, cs]
        cur = pall_ref[HALO:HALO + tm, :]
        prev = pall_ref[HALO - 1:HALO - 1 + tm, :]
        nxt = pall_ref[HALO + 1:HALO + 1 + tm, :]
        sh = (cur + mu0 * (prev - cur) + mu1 * (nxt - cur)).astype(BF16)
        if c < RKV_W // ncol:
            rkv_ref[:, cs] = sh
        else:
            lora_ref[...] = sh

    ga = _dot(hm, wg_ref[...])
    gates_ref[...] = ga[:, :GATES_W].astype(BF16)
    mla_ref[...] = ga[:, GATES_W:].astype(BF16)


def _inproj(x, g, wg, wa, mu, *, seq, tm):
    t = x.shape[0]
    nt = t // tm
    hb = tm // HALO
    last = t // HALO - 1
    kern = functools.partial(_inproj_kernel, tm=tm, seq=seq)
    return pl.pallas_call(
        kern,
        grid=(nt,),
        in_specs=[
            pl.BlockSpec((tm, D_MODEL), lambda i: (i, 0)),
            pl.BlockSpec((HALO, D_MODEL), lambda i: (jnp.maximum(i * hb - 1, 0), 0)),
            pl.BlockSpec((HALO, D_MODEL), lambda i: (jnp.minimum((i + 1) * hb, last), 0)),
            pl.BlockSpec((1, D_MODEL), lambda i: (0, 0)),
            pl.BlockSpec((D_MODEL, GATES_W + MLA_W), lambda i: (0, 0)),
            pl.BlockSpec((D_MODEL, PA_W), lambda i: (0, 0)),
            pl.BlockSpec((2, PA_W), lambda i: (0, 0)),
        ],
        out_specs=[
            pl.BlockSpec((tm, GATES_W), lambda i: (i, 0)),
            pl.BlockSpec((tm, MLA_W), lambda i: (i, 0)),
            pl.BlockSpec((tm, RKV_W), lambda i: (i, 0)),
            pl.BlockSpec((tm, LORA_W), lambda i: (i, 0)),
        ],
        out_shape=[
            jax.ShapeDtypeStruct((t, GATES_W), BF16),
            jax.ShapeDtypeStruct((t, MLA_W), BF16),
            jax.ShapeDtypeStruct((t, RKV_W), BF16),
            jax.ShapeDtypeStruct((t, LORA_W), BF16),
        ],
        scratch_shapes=[
            pltpu.VMEM((tm + 2 * HALO, D_MODEL), BF16),
            pltpu.VMEM((tm + 2 * HALO, 512), F32),
        ],
        compiler_params=_params(("parallel",)),
        name="inproj",
    )(x, x, x, g, wg, wa, mu)


def _rwkv_kernel(rf_ref, kf_ref, vf_ref, lf_ref, rb_ref, kb_ref, vb_ref, lb_ref,
                 wup_ref, base_ref, kkw_ref, kaw_ref, msk_ref, bones_ref,
                 of_ref, ob_ref, s_ref, rp_sv, o0_sv, gx_sv, slx_sv, pc_sv, *, bt, nb):
    i = pl.program_id(0)

    @pl.when(i == 0)
    def _():
        for ref in (s_ref, rp_sv, o0_sv, gx_sv, slx_sv, pc_sv):
            ref[...] = jnp.zeros_like(ref)

    n, half = UNIT, CHUNK
    nu = bt // UNIT
    bones = bones_ref[...]
    bd = bones.astype(F32)
    kkw = kkw_ref[...]
    kaw = kaw_ref[...]
    lane = lax.broadcasted_iota(jnp.int32, (n, LANES), 1)
    lane2 = lax.broadcasted_iota(jnp.int32, (2 * n, LANES), 1)
    m_strict = [msk_ref[d, 0] for d in range(2)]
    m_incl = [msk_ref[d, 1] for d in range(2)]
    m_incl_b = [m.astype(BF16) for m in m_incl]
    zero_b = jnp.zeros((n, n), BF16)
    zrow = jnp.zeros((CHUNK, LANES), BF16)

    def tile_bd(m_rc):
        return (jnp.concatenate([m_rc, m_rc], axis=0) * bd).astype(BF16)

    def pair_bd(m0, m1):
        return jnp.concatenate([jnp.concatenate([m0, zero_b], axis=1),
                                jnp.concatenate([zero_b, m1], axis=1)], axis=0)

    def tile_bd2(m_rc2):
        return pair_bd(tile_bd(m_rc2[:, :LANES]), tile_bd(m_rc2[:, LANES:]))

    def head_split(x):
        lane_x = lax.broadcasted_iota(jnp.int32, x.shape, 1)
        in_h0 = (lane_x & (LANES - 1)) < A_HEAD_DIM
        return jnp.concatenate([jnp.where(in_h0, x, 0.0), jnp.where(in_h0, 0.0, x)], axis=0).astype(BF16)

    def by_chunk(m):
        return jnp.concatenate([jnp.concatenate([m[:CHUNK], zrow], axis=0),
                                jnp.concatenate([zrow, m[CHUNK:]], axis=0)], axis=1)

    def st_preact(c):
        r_ref, k_ref, v_ref, l_ref, _ = c["refs"]
        rows, d = c["rows"], c["d"]
        lo = l_ref[rows, :].astype(F32)
        lo_t = jnp.where(lane < DECAY_LORA, jnp.tanh(lo), lo).astype(BF16)
        c["up"] = _dot(lo_t, wup_ref[d]) + base_ref[d]
        c["r"] = r_ref[rows, :].astype(F32)
        c["pk"] = k_ref[rows, :].astype(F32)
        c["v"] = v_ref[rows, :]
        c["kkraw"] = c["pk"] * kkw
        c["ss"] = _dot((c["kkraw"] * c["kkraw"]).astype(BF16), bones)

    def st_decay(c):
        d, up = c["d"], c["up"]
        sg = jax.nn.sigmoid(up)
        lw = -math.exp(-0.5) * sg[:, :LANES]
        c["a"] = sg[:, LANES:]
        lw_hi = lw.astype(BF16)
        lw_lo = (lw - lw_hi.astype(F32)).astype(BF16)
        ci2 = _dot(m_incl_b[d], jnp.concatenate([lw_hi, lw_lo], axis=1))
        ci = ci2[:, :LANES] + ci2[:, LANES:]
        ends = [ck * CHUNK + (CHUNK - 1 if d == 0 else 0) for ck in range(UNIT // CHUNK)]
        c["ctot"] = [ci[e:e + 1, :] for e in ends]
        c["lw"], c["ci"] = lw, ci
        c["crest"] = jnp.concatenate([jnp.broadcast_to(t, (CHUNK, LANES)) for t in c["ctot"]], axis=0) - ci

    def st_interact(c):
        a, pk, ci, crest = c["a"], c["pk"], c["ci"], c["crest"]
        kk = c["kkraw"] * lax.rsqrt(jnp.maximum(c["ss"], 1e-24))
        kd = pk * (1.0 + (a - 1.0) * kaw)
        b = kk * a
        at = -kk * jnp.exp(ci - c["lw"])
        rt = c["r"] * jnp.exp(ci)
        e_m = jnp.exp(-ci)
        e_r = jnp.exp(crest)
        c["at"], c["rt"] = at, rt
        c["bh"] = (b * e_r).astype(BF16)
        c["kh"] = (kd * e_r).astype(BF16)
        lhs = jnp.concatenate([at, rt], axis=0)
        rhs = jnp.concatenate([b * e_m, kd * e_m], axis=0).astype(BF16)
        lhs2 = jnp.concatenate([jnp.where(lane2 < A_HEAD_DIM, lhs, 0.0),
                                jnp.where(lane2 < A_HEAD_DIM, 0.0, lhs)], axis=0).astype(BF16)
        c["g"] = _dot_nt(lhs2, rhs)

    def st_masks(c):
        d, g2 = c["d"], c["g"]
        blk = lambda h, i, k: g2[(2 * h + i) * n:(2 * h + i + 1) * n, k * n:(k + 1) * n]
        a_bd = [blk(h, 0, 0) * m_strict[d] for h in range(2)]
        c["a_akrk"] = jnp.concatenate(
            [jnp.concatenate([blk(h, 0, 1) * m_strict[d] for h in range(2)], axis=1),
             jnp.concatenate([blk(h, 1, 1) * m_incl[d] for h in range(2)], axis=1)], axis=0).astype(BF16)
        c["a_rb"] = jnp.concatenate([blk(h, 1, 0) * m_incl[d] for h in range(2)], axis=1).astype(BF16)
        a_rc = jnp.concatenate([a[:half] + a[half:] for a in a_bd], axis=1)
        c["nm"] = a_rc
        c["p"] = _dot(a_rc.astype(BF16), pair_bd(a_bd[0].astype(BF16), a_bd[1].astype(BF16)))

    def st_level(c):
        nm, p = c["nm"], c["p"]
        tp = _dot(jnp.concatenate([nm, p], axis=0).astype(BF16), tile_bd2(p))
        c["nm"] = nm + p + tp[:half]
        c["p"] = tp[half:]

    def st_last_level(c):
        nm, p = c["nm"], c["p"]
        nm = nm + p + _dot(nm.astype(BF16), tile_bd2(p))
        c["nm_bd"] = jnp.concatenate([tile_bd(nm[:, :LANES]), tile_bd(nm[:, LANES:])], axis=1)

    def st_wo(c):
        wo = _dot(c["a_akrk"], head_split(c["v"].astype(F32)))
        c["w"], c["o0b"] = wo[:n], wo[n:]

    def st_tx(c):
        x = jnp.concatenate([c["at"], c["w"]], axis=1)
        c["tx"] = x + _dot(c["nm_bd"], head_split(x))

    def st_rx(c):
        rx = _dot(c["a_rb"], head_split(c["tx"]))
        c["o0"] = rx[:, LANES:] + c["o0b"]
        c["apu"] = c["tx"].astype(BF16)
        c["rp"] = (c["rt"] + rx[:, :LANES]).astype(BF16)

    def st_trans(c):
        gu = _dot_tn(c["apu"], by_chunk(c["bh"]))
        vk = _dot_tn(c["v"], by_chunk(c["kh"]))
        c["gx"] = [(gu[:LANES, ck * LANES:(ck + 1) * LANES] * bd).astype(BF16) for ck in range(2)]
        c["slx"] = [(gu[LANES:, ck * LANES:(ck + 1) * LANES] + vk[:, ck * LANES:(ck + 1) * LANES]) * bd
                    for ck in range(2)]
        c["pc"] = [jnp.exp(t) for t in c["ctot"]]

    nck = UNIT // CHUNK
    groups = []
    for d in range(2):
        refs = ((rf_ref, kf_ref, vf_ref, lf_ref, of_ref) if d == 0
                else (rb_ref, kb_ref, vb_ref, lb_ref, ob_ref))
        groups.append([dict(d=d, u=u, rows=slice(u * UNIT, (u + 1) * UNIT), refs=refs)
                       for u in (range(nu) if d == 0 else range(nu - 1, -1, -1))])

    stage_list = [st_preact, st_decay, st_interact, st_masks] + [st_level] * 4 + [
        st_last_level, st_wo, st_tx, st_rx, st_trans]
    work = [(st, c) for st in stage_list for grp in groups for c in grp]
    first = lax.rem(jnp.maximum(i - 1, 0), nb) == 0
    s = [jnp.where(first, 0.0, s_ref[d]) for d in range(2)]
    seqs = [[(c, ck) for c in groups[d] for ck in (range(nck) if d == 0 else range(nck - 1, -1, -1))]
            for d in range(2)]
    rounds = len(seqs[0])
    per_round = -(-len(work) // rounds)
    outs = {}
    for q in range(rounds):
        for d in range(2):
            c, ck = seqs[d][q]
            u = c["u"]
            sl = slice(u * UNIT + ck * CHUNK, u * UNIT + (ck + 1) * CHUNK)
            s_b = s[d].astype(BF16)
            outs[(d, u, ck)] = _dot_nt(rp_sv[d, sl, :], s_b) + o0_sv[d, sl, :]
            s[d] = s[d] * pc_sv[d, u, ck, 0:1, :] + _dot(s_b, gx_sv[d, u, ck]) + slx_sv[d, u, ck]
        for st, c in work[q * per_round:(q + 1) * per_round]:
            st(c)
    for grp in groups:
        for c in grp:
            c["refs"][4][c["rows"], :] = jnp.concatenate(
                [outs[(c["d"], c["u"], ck)] for ck in range(nck)], axis=0)
    s_ref[0] = s[0]
    s_ref[1] = s[1]
    for grp in groups:
        for c in grp:
            d, u = c["d"], c["u"]
            rp_sv[d, c["rows"], :] = c["rp"]
            o0_sv[d, c["rows"], :] = c["o0"]
            for ck in range(nck):
                gx_sv[d, u, ck] = c["gx"][ck]
                slx_sv[d, u, ck] = c["slx"][ck]
                pc_sv[d, u, ck] = jnp.broadcast_to(c["pc"][ck], (8, LANES))


def _rwkv_scan(rkv, lora, wup, base, kkw, kaw, masks, bones, *, nseq, seq, bt):
    t = rkv.shape[0]
    nb = seq // bt
    ng = A_WIDTH // LANES
    nsteps = nseq * ng * nb
    nu = bt // UNIT
    nck = UNIT // CHUNK

    def decode(i):
        return i // (ng * nb), (i // nb) % ng, i % nb

    def cur(i):
        return decode(jnp.minimum(i, nsteps - 1))

    def prev(i):
        return decode(jnp.maximum(i - 1, 0))

    def fw(bgj):
        b, g, j = bgj
        return b * nb + j

    def bw(bgj):
        b, g, j = bgj
        return b * nb + nb - 1 - j

    def tok_specs(tokmap, dcol):
        return [
            pl.BlockSpec((bt, LANES), lambda i: (tokmap(cur(i)), cur(i)[1])),
            pl.BlockSpec((bt, LANES), lambda i: (tokmap(cur(i)), ng + cur(i)[1])),
            pl.BlockSpec((bt, LANES), lambda i: (tokmap(cur(i)), 2 * ng + cur(i)[1])),
            pl.BlockSpec((bt, LANES), lambda i: (tokmap(cur(i)), dcol)),
        ]

    kern = functools.partial(_rwkv_kernel, bt=bt, nb=nb)
    return pl.pallas_call(
        kern,
        grid=(nsteps + 1,),
        in_specs=tok_specs(fw, 0) + tok_specs(bw, 1) + [
            pl.BlockSpec((2, None, LANES, 2 * LANES), lambda i: (0, cur(i)[1], 0, 0)),
            pl.BlockSpec((2, None, 1, 2 * LANES), lambda i: (0, cur(i)[1], 0, 0)),
            pl.BlockSpec((1, LANES), lambda i: (0, cur(i)[1])),
            pl.BlockSpec((1, LANES), lambda i: (0, cur(i)[1])),
            pl.BlockSpec((2, 2, UNIT, UNIT), lambda i: (0, 0, 0, 0)),
            pl.BlockSpec((LANES, LANES), lambda i: (0, 0)),
        ],
        out_specs=[
            pl.BlockSpec((bt, LANES), lambda i: (fw(prev(i)), prev(i)[1])),
            pl.BlockSpec((bt, LANES), lambda i: (bw(prev(i)), prev(i)[1])),
        ],
        out_shape=[jax.ShapeDtypeStruct((t, A_WIDTH), F32), jax.ShapeDtypeStruct((t, A_WIDTH), F32)],
        scratch_shapes=[
            pltpu.VMEM((2, LANES, LANES), F32),
            pltpu.VMEM((2, bt, LANES), BF16),
            pltpu.VMEM((2, bt, LANES), F32),
            pltpu.VMEM((2, nu, nck, LANES, LANES), BF16),
            pltpu.VMEM((2, nu, nck, LANES, LANES), F32),
            pltpu.VMEM((2, nu, nck, 8, LANES), F32),
        ],
        compiler_params=_params(("arbitrary",)),
        name="rwkv_scan",
    )(rkv, rkv, rkv, lora, rkv, rkv, rkv, lora, wup, base, kkw, kaw, masks, bones)


def _mla_prep_kernel(mla_ref, qg_ref, kvg_ref, wq1_ref, wq2_ref, wk_ref, wvt_ref, ones_ref, e_ref,
                     cq_ref, sq_ref, ck_ref, sk_ref, q_ref, k_ref, vt_ref):
    tm = mla_ref.shape[0]

    def norm(x, g):
        ms = jnp.mean(x * x, axis=-1, keepdims=True)
        return x * lax.rsqrt(ms + RMS_EPS) * g

    pq = mla_ref[:, 0:Q_RANK].astype(F32)
    pkv = mla_ref[:, Q_RANK:Q_RANK + KV_RANK].astype(F32)
    pkr = mla_ref[:, Q_RANK + KV_RANK:Q_RANK + KV_RANK + LANES].astype(F32)
    pkrot = mla_ref[:, Q_RANK + KV_RANK + LANES:].astype(F32)

    hq = norm(pq, qg_ref[...]).astype(BF16)
    q1 = _dot(hq, wq1_ref[...])
    q2 = _dot(hq, wq2_ref[...])
    cq = jnp.concatenate([cq_ref[...]] * B_HEADS, axis=1)
    sq = jnp.concatenate([sq_ref[...]] * B_HEADS, axis=1)
    scale = (QK_NOPE + QK_ROPE) ** -0.5 * math.log2(math.e)
    q_ref[...] = ((q1 * cq + q2 * sq) * scale).astype(BF16)

    hkv = norm(pkv, kvg_ref[...]).astype(BF16)
    krope = (pkr * ck_ref[...] + pkrot * sk_ref[...]).astype(BF16)
    k_ref[...] = (_dot(hkv, wk_ref[...]) + _dot(krope, e_ref[...])).astype(BF16)
    ones = jnp.concatenate([ones_ref[...]] * (tm // LANES), axis=1)
    vt_ref[...] = (_dot_nt(wvt_ref[...], hkv) + ones).astype(BF16)


def _mla_prep(mla, qg, kvg, wq1, wq2, wk, wvt, ones, e, cq, sq, ck, sk, *, seq, tm):
    t = mla.shape[0]
    nt = t // tm
    per = seq // tm
    kw = B_HEADS * LANES
    vr = B_HEADS * VT_ROWS
    tab = pl.BlockSpec((tm, LANES), lambda i: (i % per, 0))
    return pl.pallas_call(
        _mla_prep_kernel,
        grid=(nt,),
        in_specs=[
            pl.BlockSpec((tm, MLA_W), lambda i: (i, 0)),
            pl.BlockSpec((1, Q_RANK), lambda i: (0, 0)),
            pl.BlockSpec((1, KV_RANK), lambda i: (0, 0)),
            pl.BlockSpec((Q_RANK, kw), lambda i: (0, 0)),
            pl.BlockSpec((Q_RANK, kw), lambda i: (0, 0)),
            pl.BlockSpec((KV_RANK, kw), lambda i: (0, 0)),
            pl.BlockSpec((vr, KV_RANK), lambda i: (0, 0)),
            pl.BlockSpec((vr, LANES), lambda i: (0, 0)),
            pl.BlockSpec((LANES, kw), lambda i: (0, 0)),
            tab, tab, tab, tab,
        ],
        out_specs=[
            pl.BlockSpec((tm, kw), lambda i: (i, 0)),
            pl.BlockSpec((tm, kw), lambda i: (i, 0)),
            pl.BlockSpec((vr, tm), lambda i: (0, i)),
        ],
        out_shape=[
            jax.ShapeDtypeStruct((t, kw), BF16),
            jax.ShapeDtypeStruct((t, kw), BF16),
            jax.ShapeDtypeStruct((vr, t), BF16),
        ],
        compiler_params=_params(("parallel",)),
        name="mla_prep",
    )(mla, qg, kvg, wq1, wq2, wk, wvt, ones, e, cq, sq, ck, sk)


def _attn_kernel(q_ref, k_ref, vt_ref, o_ref):
    nh = ATTN_HEADS
    tq = ATTN_TQ
    nsub = q_ref.shape[0] // tq
    kc = min(ATTN_KCHUNK, k_ref.shape[0])
    nc = k_ref.shape[0] // kc
    items = [(qs, h) for qs in range(nsub) for h in range(nh)]

    def scores(item, i):
        qs, h = item
        s = _dot_nt(k_ref[i * kc:(i + 1) * kc, h * LANES:(h + 1) * LANES],
                    q_ref[qs * tq:(qs + 1) * tq, h * LANES:(h + 1) * LANES])
        return s, jnp.max(s.reshape(kc // 8, 8, tq), axis=0)

    outs = []
    cur = [scores(items[0], i) for i in range(nc)]
    for n, (qs, h) in enumerate(items):
        m8 = cur[0][1]
        for _, part in cur[1:]:
            m8 = jnp.maximum(m8, part)
        m = jnp.max(m8, axis=0, keepdims=True)
        nxt = []
        ot = None
        for i in range(nc):
            if n + 1 < len(items):
                nxt.append(scores(items[n + 1], i))
            p = jnp.exp2((cur[i][0] - m).astype(BF16))
            part = _dot(vt_ref[h * VT_ROWS:(h + 1) * VT_ROWS, i * kc:(i + 1) * kc], p)
            ot = part if ot is None else ot + part
        outs.append(ot[:V_DIM] / ot[V_DIM:V_DIM + 1])
        cur = nxt
        if h % 2 == 1:
            g = h // 2
            o_ref[qs * tq:(qs + 1) * tq, g * LANES:(g + 1) * LANES] = (
                jnp.concatenate(outs[-2:], axis=0).T.astype(BF16))


def _attention(q, k, vt, *, nseq, seq, tq):
    t = q.shape[0]
    nq = seq // tq
    nh = ATTN_HEADS
    ngrp = B_HEADS // nh
    return pl.pallas_call(
        _attn_kernel,
        grid=(nseq, ngrp, nq),
        in_specs=[
            pl.BlockSpec((tq, nh * LANES), lambda b, g, i: (b * nq + i, g)),
            pl.BlockSpec((seq, nh * LANES), lambda b, g, i: (b, g)),
            pl.BlockSpec((nh * VT_ROWS, seq), lambda b, g, i: (g, b)),
        ],
        out_specs=pl.BlockSpec((tq, nh * V_DIM), lambda b, g, i: (b * nq + i, g)),
        out_shape=jax.ShapeDtypeStruct((t, B_WIDTH), BF16),
        compiler_params=_params(("parallel", "parallel", "arbitrary")),
        name="mla_attention",
    )(q, k, vt)


def _mix_kernel(x_ref, of_ref, ob_ref, r_ref, k_ref, v_ref, gl_ref, gates_ref, att_ref,
                g2_ref, woa_ref, wob_ref, wout_ref, rk_ref, gng_ref, gnb_ref,
                bmean_ref, bsum_ref, o_ref):
    half = A_WIDTH // 2

    def head_stat(x, w):
        xb = x.astype(BF16)
        return jnp.concatenate([_dot(xb[:, :half], w), _dot(xb[:, half:], w)], axis=1)

    o = of_ref[...] + ob_ref[...]
    bmean = bmean_ref[...]
    mean = head_stat(o, bmean)
    dc = o - mean
    var = head_stat(dc * dc, bmean)
    on = dc * lax.rsqrt(var + GN_EPS) * gng_ref[...] + gnb_ref[...]
    r = r_ref[...].astype(F32)
    pk = k_ref[...].astype(F32)
    v = v_ref[...].astype(F32)
    bonus = head_stat(r * pk * rk_ref[...], bsum_ref[...]) * v
    g = _dot(jax.nn.sigmoid(gl_ref[...].astype(F32)).astype(BF16), g2_ref[...])
    ya = _dot(((on + bonus) * g).astype(BF16), woa_ref[...])
    yb = _dot(att_ref[...], wob_ref[...])
    sg = jax.nn.sigmoid(gates_ref[...].astype(F32))
    mix = (sg[:, :D_MODEL] * ya + sg[:, D_MODEL:] * yb).astype(BF16)
    o_ref[...] = x_ref[...] + _dot(mix, wout_ref[...])


def _mix(x, o_f, o_b, rkv, lora, gates, att, g2, woa, wob, wout, rk, gng, gnb, bmean, bsum, *, tm):
    t = x.shape[0]
    nt = t // tm

    def const(shape):
        return pl.BlockSpec(shape, lambda i: tuple(0 for _ in shape))

    return pl.pallas_call(
        _mix_kernel,
        grid=(nt,),
        in_specs=[
            pl.BlockSpec((tm, D_MODEL), lambda i: (i, 0)),
            pl.BlockSpec((tm, A_WIDTH), lambda i: (i, 0)),
            pl.BlockSpec((tm, A_WIDTH), lambda i: (i, 0)),
            pl.BlockSpec((tm, A_WIDTH), lambda i: (i, 0)),
            pl.BlockSpec((tm, A_WIDTH), lambda i: (i, 1)),
            pl.BlockSpec((tm, A_WIDTH), lambda i: (i, 2)),
            pl.BlockSpec((tm, 2 * LANES), lambda i: (i, 1)),
            pl.BlockSpec((tm, GATES_W), lambda i: (i, 0)),
            pl.BlockSpec((tm, B_WIDTH), lambda i: (i, 0)),
            const((2 * LANES, A_WIDTH)),
            const((A_WIDTH, D_MODEL)),
            const((B_WIDTH, D_MODEL)),
            const((D_MODEL, D_MODEL)),
            const((1, A_WIDTH)),
            const((1, A_WIDTH)),
            const((1, A_WIDTH)),
            const((A_WIDTH // 2, A_WIDTH // 2)),
            const((A_WIDTH // 2, A_WIDTH // 2)),
        ],
        out_specs=pl.BlockSpec((tm, D_MODEL), lambda i: (i, 0)),
        out_shape=jax.ShapeDtypeStruct((t, D_MODEL), F32),
        compiler_params=_params(("parallel",)),
        name="mix",
    )(x, o_f, o_b, rkv, rkv, rkv, lora, gates, att, g2, woa, wob, wout, rk, gng, gnb, bmean, bsum)


def _ffn_kernel(x_ref, g_ref, wg_ref, wu_ref, wd_ref, fg_ref, o_ref, *, final, fc):
    x = x_ref[...]
    ms = jnp.mean(x * x, axis=-1, keepdims=True)
    h = (x * lax.rsqrt(ms + RMS_EPS) * g_ref[...]).astype(BF16)
    acc = x
    for c in range(D_FF // fc):
        cs = slice(c * fc, (c + 1) * fc)
        gt = _dot(h, wg_ref[:, cs])
        up = _dot(h, wu_ref[:, cs])
        act = (gt * jax.nn.sigmoid(gt) * up).astype(BF16)
        acc = acc + _dot(act, wd_ref[cs, :])
    if final:
        ms2 = jnp.mean(acc * acc, axis=-1, keepdims=True)
        acc = acc * lax.rsqrt(ms2 + RMS_EPS) * fg_ref[...]
    o_ref[...] = acc


def _ffn(x, g, wg, wu, wd, fg, *, tm, final):
    t = x.shape[0]
    nt = t // tm
    kern = functools.partial(_ffn_kernel, final=final, fc=D_FF)
    return pl.pallas_call(
        kern,
        grid=(nt,),
        in_specs=[
            pl.BlockSpec((tm, D_MODEL), lambda i: (i, 0)),
            pl.BlockSpec((1, D_MODEL), lambda i: (0, 0)),
            pl.BlockSpec((D_MODEL, D_FF), lambda i: (0, 0)),
            pl.BlockSpec((D_MODEL, D_FF), lambda i: (0, 0)),
            pl.BlockSpec((D_FF, D_MODEL), lambda i: (0, 0)),
            pl.BlockSpec((1, D_MODEL), lambda i: (0, 0)),
        ],
        out_specs=pl.BlockSpec((tm, D_MODEL), lambda i: (i, 0)),
        out_shape=jax.ShapeDtypeStruct((t, D_MODEL), F32),
        compiler_params=_params(("parallel",)),
        name="ffn",
    )(x, g, wg, wu, wd, fg)


def _rot_matrix():
    half = QK_ROPE // 2
    m = np.zeros((QK_ROPE, QK_ROPE), np.float32)
    for i in range(half):
        m[i + half, i] = -1.0
        m[i, i + half] = 1.0
    return jnp.asarray(m)


def _scan_masks():
    idx = np.arange(UNIT)
    row, col = idx[:, None], idx[None, :]
    same = (row // CHUNK) == (col // CHUNK)
    out = np.zeros((2, 2, UNIT, UNIT), np.float32)
    for d, before in enumerate((col < row, col > row)):
        out[d, 0] = same & before
        out[d, 1] = same & (before | (row == col))
    return jnp.asarray(out)


def _block_diag_ones(width, block, value):
    idx = np.arange(width)
    m = ((idx[:, None] // block) == (idx[None, :] // block)).astype(np.float32) * value
    return jnp.asarray(m, dtype=BF16)


def _rope_tables(seq):
    pos = jnp.arange(seq, dtype=F32)
    inv_freq = 1.0 / (ROPE_THETA ** (jnp.arange(0, QK_ROPE, 2, dtype=F32) / QK_ROPE))
    ang = pos[:, None] * inv_freq[None, :]
    ang = jnp.concatenate([ang, ang], axis=-1)
    cos, sin = jnp.cos(ang), jnp.sin(ang)
    z = lambda w: jnp.zeros((seq, w), F32)
    cq = jnp.concatenate([jnp.ones((seq, QK_NOPE), F32), cos, z(LANES - QK_NOPE - QK_ROPE)], axis=1)
    sq = jnp.concatenate([z(QK_NOPE), sin, z(LANES - QK_NOPE - QK_ROPE)], axis=1)
    ck = jnp.concatenate([cos, z(LANES - QK_ROPE)], axis=1)
    sk = jnp.concatenate([sin, z(LANES - QK_ROPE)], axis=1)
    return cq, sq, ck, sk


def _prep_layer(l, w_in, shift_mu, decay_w2, decay_w0, iclr_a2, iclr_a0, gate_g2, w_uq, w_ukv):
    rot = _rot_matrix()
    w = w_in[l]
    off = 2 * D_MODEL
    w_gates = w[:, :off]
    w_pa = w[:, off:off + SHIFT_WIDTH]
    mu = shift_mu[l]
    o2 = off + SHIFT_WIDTH
    w_pq = w[:, o2:o2 + Q_RANK]
    w_pkv = w[:, o2 + Q_RANK:o2 + Q_RANK + KV_RANK]
    w_pkr = w[:, o2 + Q_RANK + KV_RANK:]
    zpad = jnp.zeros((D_MODEL, LANES - QK_ROPE), F32)
    w_g = jnp.concatenate([w_gates, w_pq, w_pkv, w_pkr, zpad, w_pkr @ rot, zpad], axis=1)

    def perm_pa(m):
        cuts = np.cumsum([A_WIDTH, A_WIDTH, A_WIDTH, DECAY_LORA, DECAY_LORA, ICLR_LORA, ICLR_LORA]).tolist()
        r, k, v, wf, wb, af, ab, g = jnp.split(m, cuts, axis=1)
        pad = jnp.zeros((m.shape[0], 2 * LANES - GATE_LORA), m.dtype)
        return jnp.concatenate([r, k, v, wf, af, wb, ab, g, pad], axis=1)

    w_a = perm_pa(w_pa)
    mu_p = perm_pa(mu)

    ng = A_WIDTH // LANES
    zl = jnp.zeros((DECAY_LORA, LANES), F32)
    wup = []
    base = []
    for d in range(2):
        rows_d = []
        base_d = []
        for g in range(ng):
            cs = slice(g * LANES, (g + 1) * LANES)
            top = jnp.concatenate([decay_w2[l, d][:, cs], zl], axis=1)
            bot = jnp.concatenate([zl, iclr_a2[l, d][:, cs]], axis=1)
            rows_d.append(jnp.concatenate([top, bot], axis=0))
            base_d.append(jnp.concatenate([decay_w0[l, d][cs], iclr_a0[l, d][cs]])[None, :])
        wup.append(jnp.stack(rows_d))
        base.append(jnp.stack(base_d))
    wup = jnp.stack(wup).astype(BF16)
    base = jnp.stack(base)

    g2 = jnp.concatenate([gate_g2[l], jnp.zeros((2 * LANES - GATE_LORA, A_WIDTH), F32)], axis=0)

    wq = w_uq[l].reshape(Q_RANK, B_HEADS, QK_NOPE + QK_ROPE)
    wq_n, wq_r = wq[..., :QK_NOPE], wq[..., QK_NOPE:]
    zq = jnp.zeros((Q_RANK, B_HEADS, LANES - QK_NOPE - QK_ROPE), F32)
    wq1 = jnp.concatenate([wq_n, wq_r, zq], axis=-1).reshape(Q_RANK, B_HEADS * LANES)
    wq2 = jnp.concatenate([jnp.zeros_like(wq_n), jnp.einsum('qhr,rs->qhs', wq_r, rot), zq],
                          axis=-1).reshape(Q_RANK, B_HEADS * LANES)
    wkv = w_ukv[l].reshape(KV_RANK, B_HEADS, QK_NOPE + V_DIM)
    wk_n, wv = wkv[..., :QK_NOPE], wkv[..., QK_NOPE:]
    zk = jnp.zeros((KV_RANK, B_HEADS, LANES - QK_NOPE), F32)
    wk_p = jnp.concatenate([wk_n, zk], axis=-1).reshape(KV_RANK, B_HEADS * LANES)
    zv = jnp.zeros((KV_RANK, B_HEADS, VT_ROWS - V_DIM), F32)
    wvt = jnp.concatenate([wv, zv], axis=-1).reshape(KV_RANK, B_HEADS * VT_ROWS).T
    return dict(w_g=w_g.astype(BF16), w_a=w_a.astype(BF16), mu=mu_p, wup=wup, base=base,
                g2=g2.astype(BF16), wq1=wq1.astype(BF16), wq2=wq2.astype(BF16),
                wk=wk_p.astype(BF16), wvt=wvt.astype(BF16))


def _rope_place():
    e = np.zeros((LANES, B_HEADS * LANES), np.float32)
    for h in range(B_HEADS):
        for i in range(QK_ROPE):
            e[i, h * LANES + QK_NOPE + i] = 1.0
    return jnp.asarray(e, dtype=BF16)


def _ones_rows():
    m = np.zeros((B_HEADS, VT_ROWS, LANES), np.float32)
    m[:, V_DIM:, :] = 1.0
    return jnp.asarray(m.reshape(B_HEADS * VT_ROWS, LANES))


def _trunk(x, nseq, seq, layers, consts, *, tm, tq):
    cq, sq, ck, sk = _rope_tables(seq)
    for l, p in enumerate(layers):
        gates, mla, rkv, lora = _inproj(x, p["norm_mix_g"], p["w_g"], p["w_a"], p["mu"], seq=seq, tm=tm)
        o_f, o_b = _rwkv_scan(rkv, lora, p["wup"], p["base"], p["k_k"], p["k_a"],
                              consts["masks"], consts["bones"], nseq=nseq, seq=seq, bt=min(SCAN_BT, seq))
        q, k, vt = _mla_prep(mla, p["q_norm_g"], p["kv_norm_g"], p["wq1"], p["wq2"], p["wk"], p["wvt"],
                             consts["ones_rows"], consts["e_place"], cq, sq, ck, sk, seq=seq, tm=tm)
        att = _attention(q, k, vt, nseq=nseq, seq=seq, tq=tq)
        x = _mix(x, o_f, o_b, rkv, lora, gates, att, p["g2"], p["w_oa"], p["w_ob"], p["w_out"],
                 p["r_k"], p["gn_g"], p["gn_b"], consts["bmean"], consts["bsum"], tm=tm)
        x = _ffn(x, p["norm_ffn_g"], p["ffn_g"], p["ffn_u"], p["ffn_d"], consts["final_norm_g"],
                 tm=tm, final=(l == DEPTH - 1))
    return x


def _prepare(norm_mix_g, w_in, shift_mu, decay_w2, decay_w0, iclr_a2, iclr_a0, gate_g2,
             k_k, k_a, r_k, gn_g, gn_b, w_oa, q_norm_g, w_uq, kv_norm_g, w_ukv, w_ob,
             w_out, norm_ffn_g, w_gu, w_down, final_norm_g):
    layers = []
    for l in range(DEPTH):
        p = _prep_layer(l, w_in, shift_mu, decay_w2, decay_w0, iclr_a2, iclr_a0, gate_g2, w_uq, w_ukv)
        p.update(norm_mix_g=norm_mix_g[l][None, :], k_k=k_k[l][None, :], k_a=k_a[l][None, :],
                 r_k=r_k[l].reshape(1, A_WIDTH), gn_g=gn_g[l][None, :], gn_b=gn_b[l][None, :],
                 w_oa=w_oa[l].astype(BF16), w_ob=w_ob[l].astype(BF16), w_out=w_out[l].astype(BF16),
                 q_norm_g=q_norm_g[l][None, :], kv_norm_g=kv_norm_g[l][None, :],
                 norm_ffn_g=norm_ffn_g[l][None, :], ffn_g=w_gu[l][:, :D_FF].astype(BF16),
                 ffn_u=w_gu[l][:, D_FF:].astype(BF16), ffn_d=w_down[l].astype(BF16))
        layers.append(p)
    consts = dict(masks=_scan_masks(), bones=_block_diag_ones(LANES, A_HEAD_DIM, 1.0),
                  bmean=_block_diag_ones(A_WIDTH // 2, A_HEAD_DIM, 1.0 / A_HEAD_DIM),
                  bsum=_block_diag_ones(A_WIDTH // 2, A_HEAD_DIM, 1.0), e_place=_rope_place(),
                  ones_rows=_ones_rows(), final_norm_g=final_norm_g[None, :])
    return layers, consts


def kernel(x_prompt, x_sample, norm_mix_g, w_in, shift_mu, decay_w2, decay_w0, iclr_a2, iclr_a0, gate_g2, k_k, k_a, r_k, gn_g, gn_b, w_oa, q_norm_g, w_uq, kv_norm_g, w_ukv, w_ob, w_out, norm_ffn_g, w_gu, w_down, final_norm_g):
    layers, consts = _prepare(norm_mix_g, w_in, shift_mu, decay_w2, decay_w0, iclr_a2, iclr_a0, gate_g2,
                              k_k, k_a, r_k, gn_g, gn_b, w_oa, q_norm_g, w_uq, kv_norm_g, w_ukv, w_ob,
                              w_out, norm_ffn_g, w_gu, w_down, final_norm_g)
    outs = []
    for x in (x_prompt, x_sample):
        nseq, seq, _ = x.shape
        y = _trunk(x.reshape(nseq * seq, D_MODEL), nseq, seq, layers, consts,
                   tm=min(512, seq), tq=min(2 * ATTN_TQ, seq))
        outs.append(y.reshape(nseq, seq, D_MODEL))
    return tuple(outs)
```
